```python
import math, functools
import jax, jax.numpy as jnp
from jax import lax
import numpy as np

D_MODEL = 1024
BATCH = 32
SEQ = 256
DEPTH = 2
DEC_BATCH = 8
DEC_SEQ = 4096
PAST_LEN = 512

GRID_W = 64
NORM_EPS = 1e-6
N_BRANCH = 3
NA_HEADS = D_MODEL // 128
NA_HEAD_DIM = 64
NA_WIDTH = NA_HEADS * NA_HEAD_DIM
NA_KH_MAX = 8
NA_KW = 16
NA_SCALE = NA_HEAD_DIM ** -0.5
CTX_Q_BLOCK = 128
SSM_INNER = D_MODEL
SSM_HEAD_DIM = 64
SSM_HEADS = SSM_INNER // SSM_HEAD_DIM
SSM_GROUPS = 2
SSM_STATE = 128
SSM_CONV = 5
SSM_CHUNK = 128
SSM_CONV_CH = SSM_INNER + 2 * SSM_GROUPS * SSM_STATE
POOL_WIDTH = D_MODEL // 2
POOL_WINDOWS = (2, 4, 8, 16)
POOL_GROUP = POOL_WIDTH // len(POOL_WINDOWS)
COL_Q = 0
COL_K = COL_Q + NA_WIDTH
COL_V = COL_K + NA_WIDTH
COL_Z = COL_V + NA_WIDTH
COL_XBC = COL_Z + SSM_INNER
COL_DT = COL_XBC + SSM_CONV_CH
COL_POOL = COL_DT + 2 * SSM_HEADS
IN_COLS = COL_POOL + POOL_WIDTH
D_FF = 11 * D_MODEL // 4
N_EXPERTS = 8
TOP_K = 2
D_FF_EXPERT = 7 * D_MODEL // 2
MOE_BLOCK = 256
N_DENSE = (DEPTH + 1) // 2
N_MOE = DEPTH // 2

kernel_name = 'hybrid_na_ssd_pool_prefix_diffusion_step'


def rmsnorm(x, w):
    xf = x.astype(jnp.float32)
    y = xf * lax.rsqrt(jnp.mean(xf * xf, axis=-1, keepdims=True) + NORM_EPS)
    return (y * w.astype(jnp.float32)).astype(x.dtype)


def attn_context(q, k, v):
    Bsz, Lc, H, hd = q.shape
    nb = Lc // CTX_Q_BLOCK
    qb = jnp.moveaxis(q.reshape(Bsz, nb, CTX_Q_BLOCK, H, hd), 1, 0)

    def block(qi):
        s = jnp.einsum('bqhd,bkhd->bhqk', qi, k).astype(jnp.float32) * NA_SCALE
        p = jax.nn.softmax(s, axis=-1).astype(v.dtype)
        return jnp.einsum('bhqk,bkhd->bqhd', p, v)

    o = lax.map(block, qb)
    return jnp.moveaxis(o, 0, 1).reshape(Bsz, Lc, H * hd)


def attn_latent(q, k, v, k_ctx, v_ctx, rpb):
    Bsz, L, H, hd = q.shape
    rows = L // GRID_W
    kh = min(NA_KH_MAX, rows)
    qg = q.reshape(Bsz, rows, GRID_W, H, hd)
    kg = k.reshape(Bsz, rows, GRID_W, H, hd)
    vg = v.reshape(Bsz, rows, GRID_W, H, hd)
    col = np.arange(GRID_W)
    col_start = np.clip(col - NA_KW // 2, 0, GRID_W - NA_KW)
    col_idx = col_start[:, None] + np.arange(NA_KW)[None, :]
    dcol = col_idx - col[:, None] + NA_KW - 1
    rpb_col = rpb[:, :, dcol]
    n_loc = kh * NA_KW

    def row_block(r):
        rs = jnp.clip(r - kh // 2, 0, rows - kh)
        q_r = lax.dynamic_index_in_dim(qg, r, axis=1, keepdims=False)
        k_band = lax.dynamic_slice_in_dim(kg, rs, kh, axis=1)
        v_band = lax.dynamic_slice_in_dim(vg, rs, kh, axis=1)
        k_nb = k_band[:, :, col_idx]
        v_nb = v_band[:, :, col_idx]
        drow = rs + jnp.arange(kh) - r + NA_KH_MAX - 1
        bias = jnp.transpose(jnp.take(rpb_col, drow, axis=1), (0, 2, 1, 3))
        s_loc = jnp.einsum('bqhd,biqjhd->bhqij', q_r, k_nb).astype(jnp.float32) * NA_SCALE
        s_loc = s_loc + bias[None].astype(jnp.float32)
        s_ctx = jnp.einsum('bqhd,bkhd->bhqk', q_r, k_ctx).astype(jnp.float32) * NA_SCALE
        s = jnp.concatenate([s_loc.reshape(Bsz, H, GRID_W, n_loc), s_ctx], axis=-1)
        p = jax.nn.softmax(s, axis=-1).astype(v.dtype)
        p_loc = p[..., :n_loc].reshape(Bsz, H, GRID_W, kh, NA_KW)
        return (jnp.einsum('bhqij,biqjhd->bqhd', p_loc, v_nb)
                + jnp.einsum('bhqk,bkhd->bqhd', p[..., n_loc:], v_ctx))

    o = lax.map(row_block, jnp.arange(rows))
    return jnp.moveaxis(o, 0, 1).reshape(Bsz, L, H * hd)


def dwconv_centred(x, w, b):
    C = x.shape[-1]
    pad = SSM_CONV // 2
    y = lax.conv_general_dilated(x, w[:, None, :], window_strides=(1,), padding=[(pad, SSM_CONV - 1 - pad)],
                                 dimension_numbers=('NWC', 'WIO', 'NWC'), feature_group_count=C)
    return y + b


def ssd_scan(x, dt, A, Bm, Cm, h0):
    Bsz, L, H, P = x.shape
    rep = H // Bm.shape[2]
    cl = SSM_CHUNK
    nc = L // cl
    Bh = jnp.repeat(Bm.astype(jnp.float32), rep, axis=2).reshape(Bsz, nc, cl, H, SSM_STATE)
    Ch = jnp.repeat(Cm.astype(jnp.float32), rep, axis=2).reshape(Bsz, nc, cl, H, SSM_STATE)
    xdt = (x.astype(jnp.float32) * dt[..., None]).reshape(Bsz, nc, cl, H, P)
    a_cum = jnp.cumsum((dt * A).reshape(Bsz, nc, cl, H), axis=2)
    seg = a_cum[:, :, :, None, :] - a_cum[:, :, None, :, :]
    lower = jnp.tril(jnp.ones((cl, cl), bool))[None, None, :, :, None]
    decay = jnp.exp(jnp.where(lower, seg, -jnp.inf))
    scores = jnp.einsum('bclhn,bcshn->bclsh', Ch, Bh) * decay
    y_diag = jnp.einsum('bclsh,bcshp->bclhp', scores, xdt)
    w_end = jnp.exp(a_cum[:, :, -1:, :] - a_cum)
    chunk_states = jnp.einsum('bclhn,bclh,bclhp->bchpn', Bh, w_end, xdt)
    chunk_decay = jnp.exp(a_cum[:, :, -1, :])

    def step(s, inp):
        st, dec = inp
        return s * dec[:, :, None, None] + st, s

    final, prev = lax.scan(step, h0.astype(jnp.float32),
                           (jnp.moveaxis(chunk_states, 1, 0), jnp.moveaxis(chunk_decay, 1, 0)))
    prev = jnp.moveaxis(prev, 0, 1)
    y_off = jnp.einsum('bclhn,bchpn->bclhp', Ch, prev) * jnp.exp(a_cum)[..., None]
    return (y_diag + y_off).reshape(Bsz, L, H, P), final


def ssd_branch(z, xbc, dt_raw, conv_w, conv_b, dt_bias, a_log, d_skip, norm_w, h0_f, h0_b):
    Bsz, L, _ = z.shape
    gn = SSM_GROUPS * SSM_STATE
    xbc = jax.nn.silu(dwconv_centred(xbc, conv_w, conv_b))
    xs = xbc[..., :SSM_INNER].reshape(Bsz, L, SSM_HEADS, SSM_HEAD_DIM)
    Bm = xbc[..., SSM_INNER:SSM_INNER + gn].reshape(Bsz, L, SSM_GROUPS, SSM_STATE)
    Cm = xbc[..., SSM_INNER + gn:].reshape(Bsz, L, SSM_GROUPS, SSM_STATE)
    dt = jax.nn.softplus(dt_raw.astype(jnp.float32) + dt_bias.astype(jnp.float32))
    A = -jnp.exp(a_log.astype(jnp.float32))
    y_f, s_f = ssd_scan(xs, dt[:, :, 0], A[0], Bm, Cm, h0_f)
    flip = lambda t: jnp.flip(t, axis=1)
    y_b, s_b = ssd_scan(flip(xs), flip(dt[:, :, 1]), A[1], flip(Bm), flip(Cm), h0_b)
    d_sum = (d_skip[0] + d_skip[1]).astype(jnp.float32)[:, None]
    y = y_f + flip(y_b) + d_sum * xs.astype(jnp.float32)
    y = y.reshape(Bsz, L, SSM_INNER) * jax.nn.silu(z.astype(jnp.float32))
    y = rmsnorm(y.reshape(Bsz, L, SSM_GROUPS, SSM_INNER // SSM_GROUPS),
                norm_w.reshape(SSM_GROUPS, SSM_INNER // SSM_GROUPS)).reshape(Bsz, L, SSM_INNER)
    return y.astype(z.dtype), s_f, s_b


def pool_branch(u, pool_w, pool_scale):
    Bsz, L, _ = u.shape
    uf = u.astype(jnp.float32)
    cs = jnp.pad(jnp.cumsum(uf, axis=1), ((0, 0), (1, 0), (0, 0)))
    t = np.arange(L)
    outs = []
    for gi, w in enumerate(POOL_WINDOWS):
        lo = np.clip(t - w // 2, 0, L)
        hi = np.clip(t - w // 2 + w, 0, L)
        sl = slice(gi * POOL_GROUP, (gi + 1) * POOL_GROUP)
        mean = (cs[:, hi, sl] - cs[:, lo, sl]) / (hi - lo).astype(np.float32)[None, :, None]
        outs.append(mean - uf[..., sl])
    p = jnp.stack(outs, axis=2)
    p = jnp.einsum('blgc,gcd->blgd', p, pool_w.astype(jnp.float32)).reshape(Bsz, L, POOL_WIDTH)
    return (p * pool_scale.astype(jnp.float32)).astype(u.dtype)


def swiglu(h, w1, w3, w2):
    return jnp.dot(jax.nn.silu(jnp.dot(h, w1)) * jnp.dot(h, w3), w2)


def moe_swiglu(h, router, w1, w3, w2):
    T = h.shape[0]
    logits = jnp.einsum('td,de->te', h, router).astype(jnp.float32)
    top_logit, top_idx = lax.top_k(logits, TOP_K)
    gate = jax.nn.softmax(top_logit, axis=-1)
    A = T * TOP_K
    flat_e = top_idx.reshape(A)
    order = jnp.argsort(flat_e)
    e_sorted = flat_e[order]
    tok_sorted = order // TOP_K
    gate_sorted = gate.reshape(A)[order]
    counts = jnp.bincount(flat_e, length=N_EXPERTS)
    padded = (counts + MOE_BLOCK - 1) // MOE_BLOCK * MOE_BLOCK
    start = jnp.cumsum(counts) - counts
    pad_end = jnp.cumsum(padded)
    pad_start = pad_end - padded
    dest = pad_start[e_sorted] + jnp.arange(A) - start[e_sorted]
    n_blocks = A // MOE_BLOCK + N_EXPERTS
    slot_tok = jnp.zeros((n_blocks * MOE_BLOCK,), jnp.int32).at[dest].set(tok_sorted)
    block_e = jnp.minimum(jnp.searchsorted(pad_end, jnp.arange(n_blocks) * MOE_BLOCK, side='right'), N_EXPERTS - 1)
    xb = h[slot_tok].reshape(n_blocks, MOE_BLOCK, h.shape[1])

    def expert_block(args):
        xe, e = args
        return swiglu(xe, w1[e], w3[e], w2[e])

    yb = lax.map(expert_block, (xb, block_e)).reshape(n_blocks * MOE_BLOCK, h.shape[1])
    contrib = (yb[dest] * gate_sorted[:, None].astype(yb.dtype)).astype(h.dtype)
    return jnp.zeros_like(h).at[tok_sorted].add(contrib)


def trunk_layer(x, cond, lp, ctx_k=None, ctx_v=None, s0_f=None, s0_b=None):
    Bsz, L, _ = x.shape
    m = jnp.einsum('...d,de->...e', jax.nn.silu(cond), lp['ada_w']) + lp['ada_b']
    m = m.reshape(m.shape[:-1] + (6, D_MODEL))
    if m.ndim == 2:
        sh1, sc1, g1, sh2, sc2, g2 = [m[j] for j in range(6)]
    else:
        sh1, sc1, g1, sh2, sc2, g2 = [m[:, j, None, :] for j in range(6)]
    h = rmsnorm(x, lp['norm1']) * (1 + sc1) + sh1
    proj = jnp.einsum('bld,de->ble', h, lp['w_in'])
    heads = lambda t: t.reshape(Bsz, L, NA_HEADS, NA_HEAD_DIM)
    q = rmsnorm(heads(proj[..., COL_Q:COL_K]), lp['q_norm'])
    k = rmsnorm(heads(proj[..., COL_K:COL_V]), lp['k_norm'])
    v = heads(proj[..., COL_V:COL_Z])
    if ctx_k is None:
        a = attn_context(q, k, v)
        s0_f = jnp.zeros((Bsz, SSM_HEADS, SSM_HEAD_DIM, SSM_STATE), jnp.float32)
        s0_b = s0_f
    else:
        a = attn_latent(q, k, v, ctx_k, ctx_v, lp['rpb'])
    s, s_f, s_b = ssd_branch(proj[..., COL_Z:COL_XBC], proj[..., COL_XBC:COL_DT],
                             proj[..., COL_DT:COL_POOL].reshape(Bsz, L, 2, SSM_HEADS),
                             lp['conv_w'], lp['conv_b'], lp['dt_bias'], lp['a_log'], lp['d_skip'],
                             lp['ssm_norm'], s0_f, s0_b)
    p = pool_branch(proj[..., COL_POOL:], lp['pool_w'], lp['pool_scale'])
    gates = jax.nn.sigmoid(jnp.einsum('bld,de->ble', h, lp['w_gate']) + lp['b_gate']).reshape(Bsz, L, N_BRANCH, D_MODEL)
    merged = (gates[:, :, 0] * jnp.einsum('blc,cd->bld', a, lp['wb_na'])
              + gates[:, :, 1] * jnp.einsum('blc,cd->bld', s, lp['wb_ssm'])
              + gates[:, :, 2] * jnp.einsum('blc,cd->bld', p, lp['wb_pool']))
    x = x + g1 * jnp.einsum('bld,de->ble', merged, lp['w_out'])
    h2 = rmsnorm(x, lp['norm2']) * (1 + sc2) + sh2
    x = x + g2 * lp['ffn'](h2.reshape(Bsz * L, D_MODEL)).reshape(Bsz, L, D_MODEL)
    return x, k, v, s_f, s_b


def setup_inputs(seed: int = 0) -> dict:
    key = jax.random.key(seed)
    ks = iter(jax.random.split(key, 48))
    nrm = lambda shape, scale=1.0: scale * jax.random.normal(next(ks), shape, jnp.float32)
    gain = lambda shape: 1.0 + 0.1 * jax.random.normal(next(ks), shape, jnp.float32)
    dt0 = jnp.exp(jax.random.uniform(next(ks), (DEPTH, 2, SSM_HEADS), jnp.float32, math.log(1e-3), math.log(1e-1)))
    a_init = jax.random.uniform(next(ks), (DEPTH, 2, SSM_HEADS), jnp.float32, 1.0, 16.0)
    ds = D_MODEL ** -0.5
    return {
        'x_prompt': nrm((BATCH, SEQ, D_MODEL)),
        'x_sample': nrm((DEC_BATCH, DEC_SEQ, D_MODEL)),
        'c': nrm((DEC_BATCH, D_MODEL)),
        'cache_na_k': nrm((DEC_BATCH, DEPTH, PAST_LEN, NA_HEADS, NA_HEAD_DIM)),
        'cache_na_v': nrm((DEC_BATCH, DEPTH, PAST_LEN, NA_HEADS, NA_HEAD_DIM)),
        'state_ssd_fwd': nrm((DEC_BATCH, DEPTH, SSM_HEADS, SSM_HEAD_DIM, SSM_STATE), 0.5),
        'state_ssd_bwd': nrm((DEC_BATCH, DEPTH, SSM_HEADS, SSM_HEAD_DIM, SSM_STATE), 0.5),
        'c_ctx': nrm((D_MODEL,)),
        'norm1_w': gain((DEPTH, D_MODEL)),
        'norm2_w': gain((DEPTH, D_MODEL)),
        'ada_w': nrm((DEPTH, D_MODEL, 6 * D_MODEL), 0.5 * ds),
        'ada_b': nrm((DEPTH, 6 * D_MODEL), 0.01),
        'w_in': nrm((DEPTH, D_MODEL, IN_COLS), ds),
        'na_q_norm': gain((DEPTH, NA_HEAD_DIM)),
        'na_k_norm': gain((DEPTH, NA_HEAD_DIM)),
        'na_rpb': nrm((DEPTH, NA_HEADS, 2 * NA_KH_MAX - 1, 2 * NA_KW - 1), 0.1),
        'ssm_conv_w': nrm((DEPTH, SSM_CONV, SSM_CONV_CH), SSM_CONV ** -0.5),
        'ssm_conv_b': nrm((DEPTH, SSM_CONV_CH), 0.01),
        'ssm_dt_bias': dt0 + jnp.log(-jnp.expm1(-dt0)),
        'ssm_a_log': jnp.log(a_init),
        'ssm_d': gain((DEPTH, 2, SSM_HEADS)),
        'ssm_norm_w': gain((DEPTH, SSM_INNER)),
        'pool_w': nrm((DEPTH, len(POOL_WINDOWS), POOL_GROUP, POOL_GROUP), POOL_GROUP ** -0.5),
        'pool_scale': gain((DEPTH, POOL_WIDTH)),
        'w_branch_na': nrm((DEPTH, NA_WIDTH, D_MODEL), NA_WIDTH ** -0.5),
        'w_branch_ssm': nrm((DEPTH, SSM_INNER, D_MODEL), SSM_INNER ** -0.5),
        'w_branch_pool': nrm((DEPTH, POOL_WIDTH, D_MODEL), POOL_WIDTH ** -0.5),
        'w_gate': nrm((DEPTH, D_MODEL, N_BRANCH * D_MODEL), ds),
        'b_gate': nrm((DEPTH, N_BRANCH * D_MODEL), 0.01),
        'w_out': nrm((DEPTH, D_MODEL, D_MODEL), ds),
        'ffn_w1': nrm((N_DENSE, D_MODEL, D_FF), ds),
        'ffn_w3': nrm((N_DENSE, D_MODEL, D_FF), ds),
        'ffn_w2': nrm((N_DENSE, D_FF, D_MODEL), D_FF ** -0.5),
        'moe_router': nrm((N_MOE, D_MODEL, N_EXPERTS), ds),
        'moe_w1': nrm((N_MOE, N_EXPERTS, D_MODEL, D_FF_EXPERT), ds),
        'moe_w3': nrm((N_MOE, N_EXPERTS, D_MODEL, D_FF_EXPERT), ds),
        'moe_w2': nrm((N_MOE, N_EXPERTS, D_FF_EXPERT, D_MODEL), D_FF_EXPERT ** -0.5),
    }


def reference(x_prompt, x_sample, c, cache_na_k, cache_na_v, state_ssd_fwd, state_ssd_bwd, c_ctx,
              norm1_w, norm2_w, ada_w, ada_b, w_in, na_q_norm, na_k_norm, na_rpb,
              ssm_conv_w, ssm_conv_b, ssm_dt_bias, ssm_a_log, ssm_d, ssm_norm_w,
              pool_w, pool_scale, w_branch_na, w_branch_ssm, w_branch_pool, w_gate, b_gate, w_out,
              ffn_w1, ffn_w3, ffn_w2, moe_router, moe_w1, moe_w3, moe_w2):
    layers = []
    for i in range(DEPTH):
        j = i // 2
        if i % 2 == 0:
            ffn = functools.partial(swiglu, w1=ffn_w1[j], w3=ffn_w3[j], w2=ffn_w2[j])
        else:
            ffn = functools.partial(moe_swiglu, router=moe_router[j], w1=moe_w1[j], w3=moe_w3[j], w2=moe_w2[j])
        layers.append({
            'norm1': norm1_w[i], 'norm2': norm2_w[i], 'ada_w': ada_w[i], 'ada_b': ada_b[i],
            'w_in': w_in[i], 'q_norm': na_q_norm[i], 'k_norm': na_k_norm[i], 'rpb': na_rpb[i],
            'conv_w': ssm_conv_w[i], 'conv_b': ssm_conv_b[i], 'dt_bias': ssm_dt_bias[i],
            'a_log': ssm_a_log[i], 'd_skip': ssm_d[i], 'ssm_norm': ssm_norm_w[i],
            'pool_w': pool_w[i], 'pool_scale': pool_scale[i],
            'wb_na': w_branch_na[i], 'wb_ssm': w_branch_ssm[i], 'wb_pool': w_branch_pool[i],
            'w_gate': w_gate[i], 'b_gate': b_gate[i], 'w_out': w_out[i], 'ffn': ffn,
        })

    y_prompt = x_prompt
    ks_, vs_, sfs, sbs = [], [], [], []
    for i in range(DEPTH):
        y_prompt, k_i, v_i, sf_i, sb_i = trunk_layer(y_prompt, c_ctx, layers[i])
        ks_.append(k_i)
        vs_.append(v_i)
        sfs.append(sf_i)
        sbs.append(sb_i)

    y_sample = x_sample
    for i in range(DEPTH):
        y_sample = trunk_layer(y_sample, c, layers[i], cache_na_k[:, i], cache_na_v[:, i],
                               state_ssd_fwd[:, i], state_ssd_bwd[:, i])[0]

    new_cache_na_k = jnp.stack(ks_, axis=1)
    new_cache_na_v = jnp.stack(vs_, axis=1)
    new_state_ssd_fwd = jnp.stack(sfs, axis=1)
    new_state_ssd_bwd = jnp.stack(sbs, axis=1)
    return (y_prompt, y_sample, new_cache_na_k, new_cache_na_v, new_state_ssd_fwd, new_state_ssd_bwd)
```

```python
import functools
import math

import numpy as np
import jax
import jax.numpy as jnp
from jax import lax
from jax.experimental import pallas as pl
from jax.experimental.pallas import tpu as pltpu

F32 = jnp.float32
BF16 = jnp.bfloat16

D_MODEL = 1024
GRID_W = 64
NORM_EPS = 1e-6
N_BRANCH = 3
NA_HEADS = D_MODEL // 128
NA_HEAD_DIM = 64
NA_WIDTH = NA_HEADS * NA_HEAD_DIM
NA_KH = 8
NA_KW = 16
NA_SCALE = NA_HEAD_DIM ** -0.5
SSM_INNER = D_MODEL
SSM_HEAD_DIM = 64
SSM_HEADS = SSM_INNER // SSM_HEAD_DIM
SSM_GROUPS = 2
SSM_STATE = 128
SSM_CONV = 5
SSM_CONV_CH = SSM_INNER + 2 * SSM_GROUPS * SSM_STATE
POOL_WIDTH = D_MODEL // 2
POOL_WINDOWS = (2, 4, 8, 16)
POOL_GROUP = POOL_WIDTH // len(POOL_WINDOWS)
COL_Q = 0
COL_K = COL_Q + NA_WIDTH
COL_V = COL_K + NA_WIDTH
COL_Z = COL_V + NA_WIDTH
COL_XBC = COL_Z + SSM_INNER
COL_DT = COL_XBC + SSM_CONV_CH
COL_POOL = COL_DT + 2 * SSM_HEADS
IN_COLS = COL_POOL + POOL_WIDTH
N_EXPERTS = 8

LANES = 128
VMEM_LIMIT = 56 * 1024 * 1024

SSD_CHUNK = 128
HALO = 16
LOCAL_TILE = 256
NEG = -1e30
LAT_Q_ROWS = 8
LAT_K_ROWS = 16
MOE_GROUP = 512


def _cparams(sem, vmem=VMEM_LIMIT):
    return pltpu.CompilerParams(dimension_semantics=sem, vmem_limit_bytes=vmem)


def _const_spec(shape):
    nd = len(shape)
    return pl.BlockSpec(shape, lambda *_: (0,) * nd, pipeline_mode=pl.Buffered(1))


def _mod_row(i, tm, n_ctx_tok, lat_len):
    n_ctx_tiles = n_ctx_tok // tm
    return jnp.where(i < n_ctx_tiles, 0, 1 + (i - n_ctx_tiles) // (lat_len // tm))


def _silu(x):
    return x * jax.nn.sigmoid(x)


def _split2(x):
    hi = x.astype(BF16)
    lo = (x - hi.astype(F32)).astype(BF16)
    return hi, lo


def _split3(x):
    hi = x.astype(BF16)
    r = x - hi.astype(F32)
    mid = r.astype(BF16)
    lo = (r - mid.astype(F32)).astype(BF16)
    return hi, mid, lo


def _ada_kernel(c_ref, w_ref, b_ref, o_ref):
    cb = _silu(c_ref[...]).astype(BF16)
    o_ref[...] = jnp.dot(cb, w_ref[...].astype(BF16), preferred_element_type=F32) + b_ref[...]


def ada_modulation(cond, ada_w, ada_b):
    depth, d, n = ada_w.shape
    r = cond.shape[0]
    tn = 1024
    return pl.pallas_call(
        _ada_kernel,
        out_shape=jax.ShapeDtypeStruct((depth, r, n), F32),
        grid=(depth, n // tn),
        in_specs=[pl.BlockSpec((r, d), lambda l, j: (0, 0)),
                  pl.BlockSpec((None, d, tn), lambda l, j: (l, 0, j)),
                  pl.BlockSpec((None, 1, tn), lambda l, j: (l, 0, j))],
        out_specs=pl.BlockSpec((None, r, tn), lambda l, j: (l, 0, j)),
        compiler_params=_cparams(("arbitrary", "arbitrary")),
        name="ada_modulation",
    )(cond, ada_w, ada_b.reshape(depth, 1, n))


def _modulated_norm(x, nw, shift, scale):
    y = x * lax.rsqrt(jnp.mean(x * x, axis=-1, keepdims=True) + NORM_EPS)
    return (y * nw) * (1.0 + scale) + shift


def _head_rmsnorm(acc, w_row, eblk):
    sq = (acc * acc).astype(BF16)
    half = eblk.shape[0]
    parts = [jnp.dot(sq[:, j * half:(j + 1) * half], eblk, preferred_element_type=F32)
             for j in range(NA_WIDTH // half)]
    ms = jnp.concatenate(parts, axis=-1) * (1.0 / NA_HEAD_DIM)
    return (acc * lax.rsqrt(ms + NORM_EPS)) * w_row


def _softplus(x):
    return jnp.maximum(x, 0.0) + jnp.log(1.0 + jnp.exp(-jnp.abs(x)))


def _proj_kernel(x_ref, mod_ref, nw_ref, wq_ref, wk_ref, wv_ref, wz_ref, wxbc_ref, wdt_ref, wu_ref,
                 qn_ref, kn_ref, dtb_ref, eblk_ref,
                 q_out, k_out, v_out, kb_out, vb_out, z_out, xbc_out, dt_out, u_out):
    h = _modulated_norm(x_ref[...], nw_ref[...], mod_ref[0:1, :], mod_ref[1:2, :])
    hb = h.astype(BF16)
    dot = lambda w_ref: jnp.dot(hb, w_ref[...], preferred_element_type=F32)
    eblk = eblk_ref[...]
    q = _head_rmsnorm(dot(wq_ref), qn_ref[...], eblk)
    q_out[...] = (q * NA_SCALE).astype(BF16)
    k = _head_rmsnorm(dot(wk_ref), kn_ref[...], eblk)
    k_out[...] = k
    kb_out[...] = k.astype(BF16)
    v = dot(wv_ref)
    v_out[...] = v
    vb_out[...] = v.astype(BF16)
    z_out[...] = dot(wz_ref).astype(BF16)
    xbc_out[...] = dot(wxbc_ref).astype(BF16)
    dt_out[...] = _softplus(dot(wdt_ref) + dtb_ref[...])
    u_out[...] = dot(wu_ref).astype(BF16)


def input_projection(x, mod, nw, w_in, qn, kn, dt_bias, n_ctx_tok, lat_len, tm=512):
    t, d = x.shape
    wb = w_in.astype(BF16)
    wq, wk, wv = wb[:, COL_Q:COL_K], wb[:, COL_K:COL_V], wb[:, COL_V:COL_Z]
    wz, wxbc, wu = wb[:, COL_Z:COL_XBC], wb[:, COL_XBC:COL_DT], wb[:, COL_POOL:]
    wdt = jnp.pad(wb[:, COL_DT:COL_POOL], ((0, 0), (0, LANES - 2 * SSM_HEADS)))
    dtb = jnp.pad(dt_bias.reshape(1, 2 * SSM_HEADS), ((0, 0), (0, LANES - 2 * SSM_HEADS)))
    qn_row = jnp.tile(qn, NA_HEADS).reshape(1, NA_WIDTH)
    kn_row = jnp.tile(kn, NA_HEADS).reshape(1, NA_WIDTH)
    half = 256
    eblk = jnp.asarray(np.kron(np.eye(half // NA_HEAD_DIM), np.ones((NA_HEAD_DIM, NA_HEAD_DIM))), BF16)
    row = lambda i: (i, 0)
    tok = lambda w, dt_: (pl.BlockSpec((tm, w), row), jax.ShapeDtypeStruct((t, w), dt_))
    outs = [tok(NA_WIDTH, BF16), tok(NA_WIDTH, F32), tok(NA_WIDTH, F32), tok(NA_WIDTH, BF16),
            tok(NA_WIDTH, BF16), tok(SSM_INNER, BF16), tok(SSM_CONV_CH, BF16), tok(LANES, F32),
            tok(POOL_WIDTH, BF16)]
    consts = [nw.reshape(1, d), wq, wk, wv, wz, wxbc, wdt, wu, qn_row, kn_row, dtb, eblk]
    return pl.pallas_call(
        _proj_kernel,
        out_shape=[o[1] for o in outs],
        grid=(t // tm,),
        in_specs=[pl.BlockSpec((tm, d), row),
                  pl.BlockSpec((None, 6, d), lambda i: (_mod_row(i, tm, n_ctx_tok, lat_len), 0, 0))]
                 + [_const_spec(c.shape) for c in consts],
        out_specs=[o[0] for o in outs],
        compiler_params=_cparams(("parallel",)),
        name="input_projection",
    )(x, mod, *consts)


def _gate_kernel(x_ref, mod_ref, nw_ref, wg_ref, bg_ref, g_out):
    h = _modulated_norm(x_ref[...], nw_ref[...], mod_ref[0:1, :], mod_ref[1:2, :])
    hb = h.astype(BF16)
    d = x_ref.shape[1]
    for j in range(N_BRANCH):
        sl = slice(j * d, (j + 1) * d)
        acc = jnp.dot(hb, wg_ref[:, sl], preferred_element_type=F32) + bg_ref[:, sl]
        g_out[:, sl] = jax.nn.sigmoid(acc).astype(BF16)


def branch_gates(x, mod, nw, w_gate, b_gate, n_ctx_tok, lat_len, tm=512):
    t, d = x.shape
    n = w_gate.shape[1]
    consts = [nw.reshape(1, d), w_gate.astype(BF16), b_gate.reshape(1, n)]
    return pl.pallas_call(
        _gate_kernel,
        out_shape=jax.ShapeDtypeStruct((t, n), BF16),
        grid=(t // tm,),
        in_specs=[pl.BlockSpec((tm, d), lambda i: (i, 0)),
                  pl.BlockSpec((None, 6, d), lambda i: (_mod_row(i, tm, n_ctx_tok, lat_len), 0, 0))]
                 + [_const_spec(c.shape) for c in consts],
        out_specs=pl.BlockSpec((tm, n), lambda i: (i, 0)),
        compiler_params=_cparams(("parallel",)),
        name="branch_gates",
    )(x, mod, *consts)


def _nt_dot(a, b):
    return lax.dot_general(a, b, (((1,), (1,)), ((), ())), preferred_element_type=F32)


def _head_masks(dtype):
    lane = lax.broadcasted_iota(jnp.int32, (1, LANES), 1)
    return [((lane >= j * NA_HEAD_DIM) & (lane < (j + 1) * NA_HEAD_DIM)).astype(dtype)
            for j in range(LANES // NA_HEAD_DIM)]


def _ctx_attn_kernel(q_ref, k_ref, v_ref, o_ref):
    q, k, v = q_ref[...], k_ref[...], v_ref[...]
    out = jnp.zeros(o_ref.shape, F32)
    for hm in _head_masks(BF16):
        s = _nt_dot(q * hm, k)
        p = jnp.exp(s - jnp.max(s, axis=-1, keepdims=True))
        den = jnp.sum(p, axis=-1, keepdims=True)
        out = out + jnp.dot(p.astype(BF16), v * hm, preferred_element_type=F32) / den
    o_ref[...] = out.astype(BF16)


def context_attention(q, kb, vb, n_seq, seq_len):
    blk = lambda: pl.BlockSpec((seq_len, LANES), lambda b, hp: (b, hp))
    return pl.pallas_call(
        _ctx_attn_kernel,
        out_shape=jax.ShapeDtypeStruct((n_seq * seq_len, NA_WIDTH), BF16),
        grid=(n_seq, NA_WIDTH // LANES),
        in_specs=[blk(), blk(), blk()],
        out_specs=blk(),
        compiler_params=_cparams(("parallel", "parallel")),
        name="context_attention",
    )(q, kb, vb)


def _lat_attn_kernel(q_ref, k_ref, v_ref, kc_ref, vc_ref, bias_ref, o_ref):
    rb = pl.program_id(1)
    n_rows = k_ref.shape[0] // GRID_W
    ks = jnp.clip(rb * LAT_Q_ROWS - NA_KH // 2, 0, n_rows - LAT_K_ROWS) * GRID_W
    ks = pl.multiple_of(ks, GRID_W)
    kw = k_ref[pl.ds(ks, LAT_K_ROWS * GRID_W), :]
    vw = v_ref[pl.ds(ks, LAT_K_ROWS * GRID_W), :]
    kc = kc_ref[...].astype(BF16)
    vc = vc_ref[...].astype(BF16)
    q = q_ref[...]
    out = jnp.zeros(o_ref.shape, F32)
    for j, hm in enumerate(_head_masks(BF16)):
        qh = q * hm
        s1 = _nt_dot(qh, kw) + bias_ref[j].astype(F32)
        s2 = _nt_dot(qh, kc)
        m = jnp.maximum(jnp.max(s1, axis=-1, keepdims=True), jnp.max(s2, axis=-1, keepdims=True))
        p1 = jnp.exp(s1 - m)
        p2 = jnp.exp(s2 - m)
        den = jnp.sum(p1, axis=-1, keepdims=True) + jnp.sum(p2, axis=-1, keepdims=True)
        o = (jnp.dot(p1.astype(BF16), vw * hm, preferred_element_type=F32)
             + jnp.dot(p2.astype(BF16), vc * hm, preferred_element_type=F32))
        out = out + o / den
    o_ref[...] = out.astype(BF16)


def _latent_bias_table(rpb, rows):
    col = np.arange(GRID_W)
    cs = np.clip(col - NA_KW // 2, 0, GRID_W - NA_KW)
    kc = np.arange(GRID_W)
    valid_col = (kc[None, :] >= cs[:, None]) & (kc[None, :] < cs[:, None] + NA_KW)
    dcol = np.clip(kc[None, :] - col[:, None] + NA_KW - 1, 0, 2 * NA_KW - 2)
    toe = jnp.where(valid_col[None, None], rpb[:, :, dcol], NEG)
    toe = jnp.concatenate([toe, jnp.full_like(toe[:, :1], NEG)], axis=1)
    n_d = 2 * NA_KH - 1
    tile_idx = np.full((3, LAT_Q_ROWS, LAT_K_ROWS), n_d, np.int32)
    for ty, r0 in enumerate((0, LAT_Q_ROWS, rows - LAT_Q_ROWS)):
        ks = int(np.clip(r0 - NA_KH // 2, 0, rows - LAT_K_ROWS))
        for qr in range(LAT_Q_ROWS):
            r = r0 + qr
            rs = int(np.clip(r - NA_KH // 2, 0, rows - NA_KH))
            for kr in range(LAT_K_ROWS):
                if rs <= ks + kr < rs + NA_KH:
                    tile_idx[ty, qr, kr] = ks + kr - r + NA_KH - 1
    t = jnp.take(toe, jnp.asarray(tile_idx.reshape(-1)), axis=1)
    t = t.reshape(NA_HEADS, 3, LAT_Q_ROWS, LAT_K_ROWS, GRID_W, GRID_W)
    t = jnp.transpose(t, (1, 0, 2, 4, 3, 5))
    return t.reshape(3, NA_HEADS, LAT_Q_ROWS * GRID_W, LAT_K_ROWS * GRID_W).astype(BF16)


def latent_attention(q, kb, vb, k_ctx, v_ctx, rpb, n_ctx_tok, n_seq, seq_len):
    rows = seq_len // GRID_W
    n_rb = rows // LAT_Q_ROWS
    tq = LAT_Q_ROWS * GRID_W
    lc = k_ctx.shape[1]
    bias = _latent_bias_table(rpb, rows)
    hp_n = NA_WIDTH // LANES
    heads_per = LANES // NA_HEAD_DIM
    q0 = n_ctx_tok // tq
    s0 = n_ctx_tok // seq_len
    btype = lambda rb: jnp.where(rb == 0, 0, jnp.where(rb == n_rb - 1, 2, 1))
    return pl.pallas_call(
        _lat_attn_kernel,
        out_shape=jax.ShapeDtypeStruct((n_seq * seq_len, NA_WIDTH), BF16),
        grid=(hp_n, n_rb, n_seq),
        in_specs=[pl.BlockSpec((tq, LANES), lambda hp, rb, b: (q0 + b * n_rb + rb, hp)),
                  pl.BlockSpec((seq_len, LANES), lambda hp, rb, b: (s0 + b, hp)),
                  pl.BlockSpec((seq_len, LANES), lambda hp, rb, b: (s0 + b, hp)),
                  pl.BlockSpec((None, lc, LANES), lambda hp, rb, b: (b, 0, hp)),
                  pl.BlockSpec((None, lc, LANES), lambda hp, rb, b: (b, 0, hp)),
                  pl.BlockSpec((None, heads_per, tq, LAT_K_ROWS * GRID_W),
                               lambda hp, rb, b: (btype(rb), hp, 0, 0))],
        out_specs=pl.BlockSpec((tq, LANES), lambda hp, rb, b: (b * n_rb + rb, hp)),
        compiler_params=_cparams(("arbitrary", "arbitrary", "arbitrary")),
        name="latent_attention",
    )(q, kb, vb, k_ctx, v_ctx, bias)


def _local_kernel(seq_tiles_ref, xc_ref, xp_ref, xn_ref, uc_ref, up_ref, un_ref, cw_ref, cb_ref,
                  bandc_ref, bandp_ref, bandn_ref, pw_ref, ps_ref, xbc_out, p_out, ext_ref):
    i = pl.program_id(0)
    tl = xc_ref.shape[0]
    j = seq_tiles_ref[0, i]
    nt = seq_tiles_ref[1, i]
    has_prev = (j > 0).astype(F32)
    has_next = (j < nt - 1).astype(F32)

    ext_ref[0:HALO, :] = xp_ref[...].astype(F32) * has_prev
    ext_ref[HALO:HALO + tl, :] = xc_ref[...].astype(F32)
    ext_ref[HALO + tl:, :] = xn_ref[...].astype(F32) * has_next
    pad = SSM_CONV // 2
    acc = cb_ref[...] + jnp.zeros((tl, xc_ref.shape[1]), F32)
    for k in range(SSM_CONV):
        o = HALO + k - pad
        acc = acc + ext_ref[o:o + tl, :] * cw_ref[k:k + 1, :]
    xbc_out[...] = _silu(acc).astype(BF16)

    t = j * tl + lax.broadcasted_iota(jnp.int32, (tl, 1), 0)
    seq_len = nt * tl
    uprev = (up_ref[...].astype(F32) * has_prev).astype(BF16)
    unext = (un_ref[...].astype(F32) * has_next).astype(BF16)
    for g, w in enumerate(POOL_WINDOWS):
        sl = slice(g * POOL_GROUP, (g + 1) * POOL_GROUP)
        ucur = uc_ref[:, sl]
        ssum = (jnp.dot(bandc_ref[g], ucur, preferred_element_type=F32)
                + jnp.dot(bandp_ref[g], uprev[:, sl], preferred_element_type=F32)
                + jnp.dot(bandn_ref[g], unext[:, sl], preferred_element_type=F32))
        lo = jnp.clip(t - w // 2, 0, seq_len)
        hi = jnp.clip(t - w // 2 + w, 0, seq_len)
        pooled = ssum / (hi - lo).astype(F32) - ucur.astype(F32)
        mixed = jnp.dot(pooled.astype(BF16), pw_ref[g], preferred_element_type=F32)
        p_out[:, sl] = (mixed * ps_ref[:, sl]).astype(BF16)


def _pool_bands(tl):
    t = np.arange(tl)[:, None]
    bc = np.zeros((len(POOL_WINDOWS), tl, tl), np.float32)
    bp = np.zeros((len(POOL_WINDOWS), tl, HALO), np.float32)
    bn = np.zeros((len(POOL_WINDOWS), tl, HALO), np.float32)
    for g, w in enumerate(POOL_WINDOWS):
        lo, hi = t - w // 2, t - w // 2 + w
        jc = np.arange(tl)[None, :]
        bc[g] = (jc >= lo) & (jc < hi)
        jp = np.arange(-HALO, 0)[None, :]
        bp[g] = (jp >= lo) & (jp < hi)
        jn = np.arange(tl, tl + HALO)[None, :]
        bn[g] = (jn >= lo) & (jn < hi)
    return jnp.asarray(bc, BF16), jnp.asarray(bp, BF16), jnp.asarray(bn, BF16)


def _seq_tile_table(seq_lens, tile):
    pos, cnt = [], []
    for sl in seq_lens:
        n = sl // tile
        pos += list(range(n))
        cnt += [n] * n
    return np.asarray([pos, cnt], np.int32)


def local_mixers(xbc, u, conv_w, conv_b, pool_w, pool_scale, seq_lens):
    t, c = xbc.shape
    tl = LOCAL_TILE
    table = jnp.asarray(_seq_tile_table(seq_lens, tl))
    hb = tl // HALO
    n_halo = t // HALO
    cur = lambda w: pl.BlockSpec((tl, w), lambda i, s: (i, 0))
    prev = lambda w: pl.BlockSpec((HALO, w), lambda i, s: (jnp.maximum(i * hb - 1, 0), 0))
    nxt = lambda w: pl.BlockSpec((HALO, w), lambda i, s: (jnp.minimum((i + 1) * hb, n_halo - 1), 0))
    cst = lambda a: pl.BlockSpec(a.shape, lambda i, s: (0,) * a.ndim, pipeline_mode=pl.Buffered(1))
    bc, bp, bn = _pool_bands(tl)
    consts = [conv_w, conv_b.reshape(1, c), bc, bp, bn, pool_w.astype(BF16), pool_scale.reshape(1, POOL_WIDTH)]
    return pl.pallas_call(
        _local_kernel,
        out_shape=[jax.ShapeDtypeStruct((t, c), BF16), jax.ShapeDtypeStruct((t, POOL_WIDTH), BF16)],
        grid_spec=pltpu.PrefetchScalarGridSpec(
            num_scalar_prefetch=1,
            grid=(t // tl,),
            in_specs=[cur(c), prev(c), nxt(c), cur(POOL_WIDTH), prev(POOL_WIDTH), nxt(POOL_WIDTH)]
                     + [cst(a) for a in consts],
            out_specs=[cur(c), cur(POOL_WIDTH)],
            scratch_shapes=[pltpu.VMEM((tl + 2 * HALO, c), F32)]),
        compiler_params=_cparams(("arbitrary",)),
        name="local_mixers",
    )(table, xbc, xbc, xbc, u, u, u, *consts)


def _ssd_direction(xbc_ref, dt_ref, alog_ref, expand_ref, state_ref, y_out, reverse):
    cl = xbc_ref.shape[0]
    gw = SSM_INNER // SSM_GROUPS
    lane0 = SSM_HEADS if reverse else 0
    xs = xbc_ref[:, :SSM_INNER].astype(F32)
    bm = xbc_ref[:, SSM_INNER:SSM_INNER + SSM_GROUPS * SSM_STATE]
    cm = xbc_ref[:, SSM_INNER + SSM_GROUPS * SSM_STATE:]
    dt = dt_ref[...]
    a = dt * (-jnp.exp(alog_ref[...]))
    ri = lax.broadcasted_iota(jnp.int32, (cl, cl), 0)
    ci = lax.broadcasted_iota(jnp.int32, (cl, cl), 1)
    tri = (ci >= ri) if reverse else (ci <= ri)
    trib = tri.astype(BF16)
    acum = sum(jnp.dot(trib, part, preferred_element_type=F32) for part in _split3(a))
    acum_t = acum.T
    last = 0 if reverse else cl - 1
    atot = acum[last:last + 1, :]
    stacked = jnp.concatenate([dt, jnp.exp(acum), jnp.exp(atot - acum)], axis=0)
    ex = sum(jnp.dot(part, expand_ref[...], preferred_element_type=F32) for part in _split2(stacked))
    dt_e, eacum_e, wend_e = ex[:cl], ex[cl:2 * cl], ex[2 * cl:]
    xdt = xs * dt_e
    xdt_b = xdt.astype(BF16)
    lane = lax.broadcasted_iota(jnp.int32, (1, LANES), 1)
    left = lane < SSM_HEAD_DIM
    state = state_ref[...]
    state_b = state.astype(BF16)
    y_parts = []
    for g in range(SSM_GROUPS):
        bg = bm[:, g * SSM_STATE:(g + 1) * SSM_STATE]
        cg = cm[:, g * SSM_STATE:(g + 1) * SSM_STATE]
        cb = _nt_dot(cg, bg)
        y_off = jnp.dot(cg, state_b[:, g * gw:(g + 1) * gw], preferred_element_type=F32)
        heads_g = SSM_HEADS // SSM_GROUPS
        diag = []
        for pair in range(heads_g // 2):
            ms = []
            for hh in range(2):
                hl = lane0 + g * heads_g + pair * 2 + hh
                seg = acum[:, hl:hl + 1] - acum_t[hl:hl + 1, :]
                ms.append((cb * jnp.where(tri, jnp.exp(jnp.minimum(seg, 0.0)), 0.0)).astype(BF16))
            c0 = g * gw + pair * LANES
            xp = xdt_b[:, c0:c0 + LANES]
            rhs = jnp.concatenate([jnp.where(left, xp, jnp.zeros_like(xp)),
                                   jnp.where(left, jnp.zeros_like(xp), xp)], axis=0)
            diag.append(jnp.dot(jnp.concatenate(ms, axis=1), rhs, preferred_element_type=F32))
        y_parts.append(jnp.concatenate(diag, axis=1) + y_off * eacum_e[:, g * gw:(g + 1) * gw])
        xw = (xdt[:, g * gw:(g + 1) * gw] * wend_e[:, g * gw:(g + 1) * gw]).astype(BF16)
        upd = jnp.dot(bg.astype(F32).T.astype(BF16), xw, preferred_element_type=F32)
        state_ref[:, g * gw:(g + 1) * gw] = (state[:, g * gw:(g + 1) * gw]
                                             * eacum_e[last:last + 1, g * gw:(g + 1) * gw] + upd)
    y_out[...] = jnp.concatenate(y_parts, axis=1)


def _ssd_kernel(tab_ref, xf_ref, xb_ref, dtf_ref, dtb_ref, alog_ref, ef_ref, eb_ref, h0f_ref, h0b_ref,
                yf_out, yb_out, sf_out, sb_out, state_f, state_b):
    g = pl.program_id(0)

    @pl.when(tab_ref[1, g] == 1)
    def _():
        state_f[...] = h0f_ref[...].T
        state_b[...] = h0b_ref[...].T

    _ssd_direction(xf_ref, dtf_ref, alog_ref, ef_ref, state_f, yf_out, reverse=False)
    _ssd_direction(xb_ref, dtb_ref, alog_ref, eb_ref, state_b, yb_out, reverse=True)

    @pl.when(tab_ref[2, g] == 1)
    def _():
        sf_out[...] = state_f[...].T
        sb_out[...] = state_b[...].T


def _ssd_tables(seq_lens, cl):
    bwd, first, last, seq = [], [], [], []
    start = 0
    for s, sl in enumerate(seq_lens):
        nc = sl // cl
        for c in range(nc):
            bwd.append(start + nc - 1 - c)
            first.append(int(c == 0))
            last.append(int(c == nc - 1))
            seq.append(s)
        start += nc
    return np.asarray([bwd, first, last, seq], np.int32)


def ssd_scan(xbc, dt, a_log, h0_f, h0_b, seq_lens):
    t = xbc.shape[0]
    cl = SSD_CHUNK
    n_seq = len(seq_lens)
    tab = jnp.asarray(_ssd_tables(seq_lens, cl))
    alog = jnp.pad(a_log.reshape(1, 2 * SSM_HEADS), ((0, 0), (0, LANES - 2 * SSM_HEADS)))
    ef = np.zeros((LANES, SSM_INNER), np.float32)
    eb = np.zeros((LANES, SSM_INNER), np.float32)
    for h in range(SSM_HEADS):
        ef[h, h * SSM_HEAD_DIM:(h + 1) * SSM_HEAD_DIM] = 1.0
        eb[SSM_HEADS + h, h * SSM_HEAD_DIM:(h + 1) * SSM_HEAD_DIM] = 1.0
    fwd = lambda w: pl.BlockSpec((cl, w), lambda g, tb: (g, 0))
    bwd = lambda w: pl.BlockSpec((cl, w), lambda g, tb: (tb[0, g], 0))
    cst = lambda shape: pl.BlockSpec(shape, lambda g, tb: (0,) * len(shape), pipeline_mode=pl.Buffered(1))
    st = lambda: pl.BlockSpec((None, SSM_INNER, SSM_STATE), lambda g, tb: (tb[3, g], 0, 0))
    return pl.pallas_call(
        _ssd_kernel,
        out_shape=[jax.ShapeDtypeStruct((t, SSM_INNER), F32), jax.ShapeDtypeStruct((t, SSM_INNER), F32),
                   jax.ShapeDtypeStruct((n_seq, SSM_INNER, SSM_STATE), F32),
                   jax.ShapeDtypeStruct((n_seq, SSM_INNER, SSM_STATE), F32)],
        grid_spec=pltpu.PrefetchScalarGridSpec(
            num_scalar_prefetch=1,
            grid=(t // cl,),
            in_specs=[fwd(SSM_CONV_CH), bwd(SSM_CONV_CH), fwd(LANES), bwd(LANES),
                      cst((1, LANES)), cst((LANES, SSM_INNER)), cst((LANES, SSM_INNER)), st(), st()],
            out_specs=[fwd(SSM_INNER), bwd(SSM_INNER), st(), st()],
            scratch_shapes=[pltpu.VMEM((SSM_STATE, SSM_INNER), F32), pltpu.VMEM((SSM_STATE, SSM_INNER), F32)]),
        compiler_params=_cparams(("arbitrary",)),
        name="ssd_scan",
    )(tab, xbc, xbc, dt, dt, alog, jnp.asarray(ef, BF16), jnp.asarray(eb, BF16), h0_f, h0_b)


def _merge_kernel(x_ref, mod_ref, a_ref, yf_ref, yb_ref, xs_ref, z_ref, p_ref, g_ref,
                  dsum_ref, snw_ref, wna_ref, wssm_ref, wpool_ref, wout_ref, n2_ref, *rest,
                  with_router):
    if with_router:
        router_ref, x1_out, h2_out, logit_out = rest
    else:
        x1_out, h2_out = rest
    d = x_ref.shape[1]
    y = yf_ref[...] + yb_ref[...] + dsum_ref[...] * xs_ref[...].astype(F32)
    y = y * _silu(z_ref[...].astype(F32))
    gw = SSM_INNER // SSM_GROUPS
    parts = []
    for g in range(SSM_GROUPS):
        yg = y[:, g * gw:(g + 1) * gw]
        yg = yg * lax.rsqrt(jnp.mean(yg * yg, axis=-1, keepdims=True) + NORM_EPS)
        parts.append(yg * snw_ref[:, g * gw:(g + 1) * gw])
    s = jnp.concatenate(parts, axis=1).astype(BF16)
    gate = lambda j: g_ref[:, j * d:(j + 1) * d].astype(F32)
    merged = (gate(0) * jnp.dot(a_ref[...], wna_ref[...], preferred_element_type=F32)
              + gate(1) * jnp.dot(s, wssm_ref[...], preferred_element_type=F32)
              + gate(2) * jnp.dot(p_ref[...], wpool_ref[...], preferred_element_type=F32))
    x1 = x_ref[...] + mod_ref[2:3, :] * jnp.dot(merged.astype(BF16), wout_ref[...],
                                                preferred_element_type=F32)
    x1_out[...] = x1
    h2 = _modulated_norm(x1, n2_ref[...], mod_ref[3:4, :], mod_ref[4:5, :])
    if with_router:
        h2_out[...] = h2
        hh, hl = _split2(h2)
        rh, rl = router_ref[0], router_ref[1]
        logit_out[...] = (jnp.dot(hh, rh, preferred_element_type=F32)
                          + jnp.dot(hl, rh, preferred_element_type=F32)
                          + jnp.dot(hh, rl, preferred_element_type=F32))
    else:
        h2_out[...] = h2.astype(BF16)


def merge_branches(x, mod, a, yf, yb, xbc, z, p, gates, d_sum, ssm_norm_w, wb_na, wb_ssm, wb_pool, w_out,
                   norm2_w, router, n_ctx_tok, lat_len, tm=512):
    t, d = x.shape
    row = lambda i: (i, 0)
    consts = [jnp.repeat(d_sum, SSM_HEAD_DIM).reshape(1, SSM_INNER), ssm_norm_w.reshape(1, SSM_INNER),
              wb_na.astype(BF16), wb_ssm.astype(BF16), wb_pool.astype(BF16), w_out.astype(BF16),
              norm2_w.reshape(1, d)]
    out_shape = [jax.ShapeDtypeStruct((t, d), F32)]
    out_specs = [pl.BlockSpec((tm, d), row)]
    with_router = router is not None
    if with_router:
        rpad = jnp.pad(router, ((0, 0), (0, LANES - router.shape[1])))
        rh = rpad.astype(BF16)
        rl = (rpad - rh.astype(F32)).astype(BF16)
        consts.append(jnp.stack([rh, rl]))
        out_shape += [jax.ShapeDtypeStruct((t, d), F32), jax.ShapeDtypeStruct((t, LANES), F32)]
        out_specs += [pl.BlockSpec((tm, d), row), pl.BlockSpec((tm, LANES), row)]
    else:
        out_shape += [jax.ShapeDtypeStruct((t, d), BF16)]
        out_specs += [pl.BlockSpec((tm, d), row)]
    return pl.pallas_call(
        functools.partial(_merge_kernel, with_router=with_router),
        out_shape=out_shape,
        grid=(t // tm,),
        in_specs=[pl.BlockSpec((tm, d), row),
                  pl.BlockSpec((None, 6, d), lambda i: (_mod_row(i, tm, n_ctx_tok, lat_len), 0, 0)),
                  pl.BlockSpec((tm, NA_WIDTH), row), pl.BlockSpec((tm, SSM_INNER), row),
                  pl.BlockSpec((tm, SSM_INNER), row), pl.BlockSpec((tm, SSM_INNER), row),
                  pl.BlockSpec((tm, SSM_INNER), row), pl.BlockSpec((tm, POOL_WIDTH), row),
                  pl.BlockSpec((tm, N_BRANCH * d), row)]
                 + [_const_spec(c.shape) for c in consts],
        out_specs=out_specs,
        compiler_params=_cparams(("parallel",)),
        name="merge_branches",
    )(x, mod, a, yf, yb, xbc, z, p, gates, *consts)


def _ffn_kernel(h_ref, x_ref, mod_ref, w1_ref, w3_ref, w2_ref, o_ref, acc_ref, *, tf):
    f = w1_ref.shape[1]
    hb = h_ref[...]
    for c in range(f // tf):
        sl = slice(c * tf, (c + 1) * tf)
        u = (_silu(jnp.dot(hb, w1_ref[:, sl], preferred_element_type=F32))
             * jnp.dot(hb, w3_ref[:, sl], preferred_element_type=F32))
        part = jnp.dot(u.astype(BF16), w2_ref[sl, :], preferred_element_type=F32)
        if c == 0:
            acc_ref[...] = part
        else:
            acc_ref[...] += part
    o_ref[...] = x_ref[...] + mod_ref[5:6, :] * acc_ref[...]


def dense_ffn(h2, x1, mod, w1, w3, w2, n_ctx_tok, lat_len, tm=512, tf=256):
    t, d = x1.shape
    consts = [w1.astype(BF16), w3.astype(BF16), w2.astype(BF16)]
    return pl.pallas_call(
        functools.partial(_ffn_kernel, tf=tf),
        out_shape=jax.ShapeDtypeStruct((t, d), F32),
        grid=(t // tm,),
        in_specs=[pl.BlockSpec((tm, d), lambda i: (i, 0)), pl.BlockSpec((tm, d), lambda i: (i, 0)),
                  pl.BlockSpec((None, 6, d), lambda i: (_mod_row(i, tm, n_ctx_tok, lat_len), 0, 0))]
                 + [_const_spec(c.shape) for c in consts],
        out_specs=pl.BlockSpec((tm, d), lambda i: (i, 0)),
        scratch_shapes=[pltpu.VMEM((tm, d), F32)],
        compiler_params=_cparams(("parallel",)),
        name="dense_ffn",
    )(h2, x1, mod, *consts)


def _route_kernel(logit_ref, info_out, gate_out, count_out, base_ref):
    i = pl.program_id(0)

    @pl.when(i == 0)
    def _():
        base_ref[...] = jnp.zeros_like(base_ref)

    lg = logit_ref[...]
    tm = lg.shape[0]
    lane = lax.broadcasted_iota(jnp.int32, lg.shape, 1)
    valid = lane < N_EXPERTS
    lg = jnp.where(valid, lg, -jnp.inf)
    m1 = jnp.max(lg, axis=-1, keepdims=True)
    e1 = jnp.min(jnp.where(lg == m1, lane, LANES), axis=-1, keepdims=True)
    lg2 = jnp.where(lane == e1, -jnp.inf, lg)
    m2 = jnp.max(lg2, axis=-1, keepdims=True)
    e2 = jnp.min(jnp.where(lg2 == m2, lane, LANES), axis=-1, keepdims=True)
    g1 = 1.0 / (1.0 + jnp.exp(m2 - m1))
    oh1 = (lane == e1).astype(F32)
    oh2 = (lane == e2).astype(F32)
    oh = oh1 + oh2
    ri = lax.broadcasted_iota(jnp.int32, (tm, tm), 0)
    ci = lax.broadcasted_iota(jnp.int32, (tm, tm), 1)
    before = jnp.dot((ci < ri).astype(BF16), oh.astype(BF16), preferred_element_type=F32) + base_ref[...]
    r1 = jnp.sum(oh1 * before, axis=-1, keepdims=True)
    r2 = jnp.sum(oh2 * before, axis=-1, keepdims=True)
    lane8 = lax.broadcasted_iota(jnp.int32, (tm, 8), 1)
    info = jnp.where(lane8 == 0, e1, jnp.where(lane8 == 1, e2,
                     jnp.where(lane8 == 2, r1.astype(jnp.int32), r2.astype(jnp.int32))))
    info_out[...] = info
    gate_out[...] = jnp.where(lane8 == 0, g1, 1.0 - g1)
    base_ref[...] += jnp.sum(oh, axis=0, keepdims=True)
    count_out[...] = base_ref[...].astype(jnp.int32)


def route_tokens(logits, tm=512):
    t = logits.shape[0]
    return pl.pallas_call(
        _route_kernel,
        out_shape=[jax.ShapeDtypeStruct((t, 8), jnp.int32), jax.ShapeDtypeStruct((t, 8), F32),
                   jax.ShapeDtypeStruct((1, LANES), jnp.int32)],
        grid=(t // tm,),
        in_specs=[pl.BlockSpec((tm, LANES), lambda i: (i, 0))],
        out_specs=[pl.BlockSpec((tm, 8), lambda i: (i, 0)), pl.BlockSpec((tm, 8), lambda i: (i, 0)),
                   pl.BlockSpec((1, LANES), lambda i: (0, 0))],
        scratch_shapes=[pltpu.VMEM((1, LANES), F32)],
        compiler_params=_cparams(("arbitrary",)),
        name="route_tokens",
    )(logits)


def _dispatch_kernel(dest_ref, h_ref, xb_in, xb_out, sem):
    del xb_in
    tm = h_ref.shape[0]

    def copy(n):
        return pltpu.make_async_copy(h_ref.at[pl.ds(n // 2, 1)], xb_out.at[pl.ds(dest_ref[0, n], 1)], sem)

    def start(n, c):
        copy(n).start()
        return c

    def wait(n, c):
        copy(n).wait()
        return c

    lax.fori_loop(0, 2 * tm, start, 0)
    lax.fori_loop(0, 2 * tm, wait, 0)


def dispatch_rows(h2, dest, n_slots, tm=256):
    t, d = h2.shape
    dest_tiles = dest.reshape(t // tm, 1, 2 * tm)
    xb0 = jnp.zeros((n_slots, d), h2.dtype)
    return pl.pallas_call(
        _dispatch_kernel,
        out_shape=jax.ShapeDtypeStruct((n_slots, d), h2.dtype),
        grid=(t // tm,),
        in_specs=[pl.BlockSpec((None, 1, 2 * tm), lambda i: (i, 0, 0), memory_space=pltpu.SMEM),
                  pl.BlockSpec((tm, d), lambda i: (i, 0)),
                  pl.BlockSpec(memory_space=pl.ANY)],
        out_specs=pl.BlockSpec(memory_space=pl.ANY),
        scratch_shapes=[pltpu.SemaphoreType.DMA(())],
        input_output_aliases={2: 0},
        compiler_params=_cparams(("arbitrary",)),
        name="moe_dispatch",
    )(dest_tiles, h2, xb0)


def _expert_kernel(ge_ref, x_ref, w1_ref, w3_ref, w2_ref, o_ref, acc_ref):
    f = pl.program_id(1)
    xb = x_ref[...].astype(BF16)
    u = (_silu(jnp.dot(xb, w1_ref[...], preferred_element_type=F32))
         * jnp.dot(xb, w3_ref[...], preferred_element_type=F32))
    part = jnp.dot(u.astype(BF16), w2_ref[...], preferred_element_type=F32)

    @pl.when(f == 0)
    def _():
        acc_ref[...] = part

    @pl.when(f > 0)
    def _():
        acc_ref[...] += part

    @pl.when(f == pl.num_programs(1) - 1)
    def _():
        o_ref[...] = acc_ref[...]


def expert_ffn(xb, group_expert, w1, w3, w2, tf=512):
    n_slots, d = xb.shape
    ff = w1.shape[2]
    tg = MOE_GROUP
    return pl.pallas_call(
        _expert_kernel,
        out_shape=jax.ShapeDtypeStruct((n_slots, d), F32),
        grid_spec=pltpu.PrefetchScalarGridSpec(
            num_scalar_prefetch=1,
            grid=(n_slots // tg, ff // tf),
            in_specs=[pl.BlockSpec((tg, d), lambda g, f, ge: (g, 0)),
                      pl.BlockSpec((None, d, tf), lambda g, f, ge: (ge[g], 0, f)),
                      pl.BlockSpec((None, d, tf), lambda g, f, ge: (ge[g], 0, f)),
                      pl.BlockSpec((None, tf, d), lambda g, f, ge: (ge[g], f, 0))],
            out_specs=pl.BlockSpec((tg, d), lambda g, f, ge: (g, 0)),
            scratch_shapes=[pltpu.VMEM((tg, d), F32)]),
        compiler_params=_cparams(("arbitrary", "arbitrary")),
        name="expert_ffn",
    )(group_expert, xb, w1.astype(BF16), w3.astype(BF16), w2.astype(BF16))


def _combine_kernel(dest_ref, yb_ref, x_ref, mod_ref, gate_ref, o_ref, rows_ref, sem):
    tm = x_ref.shape[0]

    def copy(n):
        return pltpu.make_async_copy(yb_ref.at[pl.ds(dest_ref[0, n], 1)],
                                     rows_ref.at[n % 2, pl.ds(n // 2, 1)], sem)

    def start(n, c):
        copy(n).start()
        return c

    def wait(n, c):
        copy(n).wait()
        return c

    lax.fori_loop(0, 2 * tm, start, 0)
    lax.fori_loop(0, 2 * tm, wait, 0)
    y = rows_ref[0] * gate_ref[:, 0:1] + rows_ref[1] * gate_ref[:, 1:2]
    o_ref[...] = x_ref[...] + mod_ref[5:6, :] * y


def combine_rows(yb, dest, gates, x1, mod, n_ctx_tok, lat_len, tm=256):
    t, d = x1.shape
    dest_tiles = dest.reshape(t // tm, 1, 2 * tm)
    return pl.pallas_call(
        _combine_kernel,
        out_shape=jax.ShapeDtypeStruct((t, d), F32),
        grid=(t // tm,),
        in_specs=[pl.BlockSpec((None, 1, 2 * tm), lambda i: (i, 0, 0), memory_space=pltpu.SMEM),
                  pl.BlockSpec(memory_space=pl.ANY),
                  pl.BlockSpec((tm, d), lambda i: (i, 0)),
                  pl.BlockSpec((None, 6, d), lambda i: (_mod_row(i, tm, n_ctx_tok, lat_len), 0, 0)),
                  pl.BlockSpec((tm, 8), lambda i: (i, 0))],
        out_specs=pl.BlockSpec((tm, d), lambda i: (i, 0)),
        scratch_shapes=[pltpu.VMEM((2, tm, d), F32), pltpu.SemaphoreType.DMA(())],
        compiler_params=_cparams(("arbitrary",)),
        name="moe_combine",
    )(dest_tiles, yb, x1, mod, gates)


def moe_ffn(h2, logits, x1, mod, w1, w3, w2, n_ctx_tok, lat_len):
    t = h2.shape[0]
    info, gates, counts = route_tokens(logits)
    counts = counts[0, :N_EXPERTS]
    padded = (counts + MOE_GROUP - 1) // MOE_GROUP * MOE_GROUP
    pad_end = jnp.cumsum(padded)
    pad_start = pad_end - padded
    experts = info[:, 0:2]
    dest = jnp.sum(jnp.where(experts[..., None] == jnp.arange(N_EXPERTS), pad_start, 0), axis=-1) + info[:, 2:4]
    n_groups = 2 * t // MOE_GROUP + N_EXPERTS
    group_start = jnp.arange(n_groups, dtype=jnp.int32) * MOE_GROUP
    group_expert = jnp.minimum(jnp.sum(group_start[:, None] >= pad_end[None, :], axis=-1), N_EXPERTS - 1)
    xb = dispatch_rows(h2, dest.astype(jnp.int32), n_groups * MOE_GROUP)
    yb = expert_ffn(xb, group_expert.astype(jnp.int32), w1, w3, w2)
    return combine_rows(yb, dest.astype(jnp.int32), gates, x1, mod, n_ctx_tok, lat_len)


def kernel(x_prompt, x_sample, c, cache_na_k, cache_na_v, state_ssd_fwd, state_ssd_bwd, c_ctx, norm1_w, norm2_w, ada_w, ada_b, w_in, na_q_norm, na_k_norm, na_rpb, ssm_conv_w, ssm_conv_b, ssm_dt_bias, ssm_a_log, ssm_d, ssm_norm_w, pool_w, pool_scale, w_branch_na, w_branch_ssm, w_branch_pool, w_gate, b_gate, w_out, ffn_w1, ffn_w3, ffn_w2, moe_router, moe_w1, moe_w3, moe_w2):
    n_ctx, ctx_len, d = x_prompt.shape
    n_lat, lat_len, _ = x_sample.shape
    depth = ada_w.shape[0]
    n_ctx_tok = n_ctx * ctx_len
    seq_lens = (ctx_len,) * n_ctx + (lat_len,) * n_lat
    lc = cache_na_k.shape[2]

    x = jnp.concatenate([x_prompt.reshape(n_ctx_tok, d), x_sample.reshape(n_lat * lat_len, d)], axis=0)
    cond = jnp.concatenate([c_ctx[None, :], c], axis=0)
    cond = jnp.pad(cond, ((0, -cond.shape[0] % 8), (0, 0)))
    mods = ada_modulation(cond, ada_w, ada_b).reshape(depth, cond.shape[0], 6, d)
    zero_state = jnp.zeros((n_ctx, SSM_INNER, SSM_STATE), F32)

    ks, vs, sfs, sbs = [], [], [], []
    for i in range(depth):
        mod = mods[i]
        q, k, v, kb, vb, z, xbc, dt, u = input_projection(
            x, mod, norm1_w[i], w_in[i], na_q_norm[i], na_k_norm[i], ssm_dt_bias[i], n_ctx_tok, lat_len)
        gates = branch_gates(x, mod, norm1_w[i], w_gate[i], b_gate[i], n_ctx_tok, lat_len)
        a_ctx = context_attention(q, kb, vb, n_ctx, ctx_len)
        a_lat = latent_attention(q, kb, vb, cache_na_k[:, i].reshape(n_lat, lc, NA_WIDTH),
                                 cache_na_v[:, i].reshape(n_lat, lc, NA_WIDTH), na_rpb[i],
                                 n_ctx_tok, n_lat, lat_len)
        a = jnp.concatenate([a_ctx, a_lat], axis=0)
        xbc_c, p = local_mixers(xbc, u, ssm_conv_w[i], ssm_conv_b[i], pool_w[i], pool_scale[i], seq_lens)
        h0_f = jnp.concatenate([zero_state, state_ssd_fwd[:, i].reshape(n_lat, SSM_INNER, SSM_STATE)], axis=0)
        h0_b = jnp.concatenate([zero_state, state_ssd_bwd[:, i].reshape(n_lat, SSM_INNER, SSM_STATE)], axis=0)
        yf, yb, s_f, s_b = ssd_scan(xbc_c, dt, ssm_a_log[i], h0_f, h0_b, seq_lens)
        j = i // 2
        router = moe_router[j] if i % 2 == 1 else None
        res = merge_branches(x, mod, a, yf, yb, xbc_c, z, p, gates, ssm_d[i, 0] + ssm_d[i, 1],
                             ssm_norm_w[i], w_branch_na[i], w_branch_ssm[i], w_branch_pool[i], w_out[i],
                             norm2_w[i], router, n_ctx_tok, lat_len)
        if i % 2 == 0:
            x1, h2 = res
            x = dense_ffn(h2, x1, mod, ffn_w1[j], ffn_w3[j], ffn_w2[j], n_ctx_tok, lat_len)
        else:
            x1, h2, logits = res
            x = moe_ffn(h2, logits, x1, mod, moe_w1[j], moe_w3[j], moe_w2[j], n_ctx_tok, lat_len)
        ks.append(k[:n_ctx_tok].reshape(n_ctx, ctx_len, NA_HEADS, NA_HEAD_DIM))
        vs.append(v[:n_ctx_tok].reshape(n_ctx, ctx_len, NA_HEADS, NA_HEAD_DIM))
        sfs.append(s_f[:n_ctx].reshape(n_ctx, SSM_HEADS, SSM_HEAD_DIM, SSM_STATE))
        sbs.append(s_b[:n_ctx].reshape(n_ctx, SSM_HEADS, SSM_HEAD_DIM, SSM_STATE))

    y_prompt = x[:n_ctx_tok].reshape(n_ctx, ctx_len, d)
    y_sample = x[n_ctx_tok:].reshape(n_lat, lat_len, d)
    return (y_prompt, y_sample, jnp.stack(ks, axis=1), jnp.stack(vs, axis=1),
            jnp.stack(sfs, axis=1), jnp.stack(sbs, axis=1))
```

```python
import functools
import math

import numpy as np
import jax
import jax.numpy as jnp
from jax import lax
from jax.experimental import pallas as pl
from jax.experimental.pallas import tpu as pltpu

F32 = jnp.float32
BF16 = jnp.bfloat16

D_MODEL = 1024
GRID_W = 64
NORM_EPS = 1e-6
N_BRANCH = 3
NA_HEADS = D_MODEL // 128
NA_HEAD_DIM = 64
NA_WIDTH = NA_HEADS * NA_HEAD_DIM
NA_KH = 8
NA_KW = 16
NA_SCALE = NA_HEAD_DIM ** -0.5
SSM_INNER = D_MODEL
SSM_HEAD_DIM = 64
SSM_HEADS = SSM_INNER // SSM_HEAD_DIM
SSM_GROUPS = 2
SSM_STATE = 128
SSM_CONV = 5
SSM_CONV_CH = SSM_INNER + 2 * SSM_GROUPS * SSM_STATE
POOL_WIDTH = D_MODEL // 2
POOL_WINDOWS = (2, 4, 8, 16)
POOL_GROUP = POOL_WIDTH // len(POOL_WINDOWS)
COL_Q = 0
COL_K = COL_Q + NA_WIDTH
COL_V = COL_K + NA_WIDTH
COL_Z = COL_V + NA_WIDTH
COL_XBC = COL_Z + SSM_INNER
COL_DT = COL_XBC + SSM_CONV_CH
COL_POOL = COL_DT + 2 * SSM_HEADS
IN_COLS = COL_POOL + POOL_WIDTH
N_EXPERTS = 8

LANES = 128
VMEM_LIMIT = 56 * 1024 * 1024

SSD_CHUNK = 128
HALO = 16
LOCAL_TILE = 256
CONV_ROWS = 64
NEG = -1e30
LAT_Q_ROWS = 8
LAT_K_ROWS = 16
MOE_GROUP = 512
ROW_TILES = D_MODEL // LANES


def _cparams(sem, vmem=VMEM_LIMIT):
    return pltpu.CompilerParams(dimension_semantics=sem, vmem_limit_bytes=vmem)


def _const_spec(shape):
    nd = len(shape)
    return pl.BlockSpec(shape, lambda *_: (0,) * nd, pipeline_mode=pl.Buffered(1))


def _mod_row(i, tm, n_ctx_tok, lat_len):
    n_ctx_tiles = n_ctx_tok // tm
    return jnp.where(i < n_ctx_tiles, 0, 1 + (i - n_ctx_tiles) // (lat_len // tm))


def _silu(x):
    return x * jax.nn.sigmoid(x)


def _store_token_major(ref, val):
    rows = val.shape[0]
    for k in range(ROW_TILES):
        ref[pl.ds(k, rows, stride=ROW_TILES), :] = val[:, k * LANES:(k + 1) * LANES]


def _load_token_major(ref, rows, k):
    return ref[pl.ds(k, rows, stride=ROW_TILES), :]


def _split2(x):
    hi = x.astype(BF16)
    lo = (x - hi.astype(F32)).astype(BF16)
    return hi, lo


def _split3(x):
    hi = x.astype(BF16)
    r = x - hi.astype(F32)
    mid = r.astype(BF16)
    lo = (r - mid.astype(F32)).astype(BF16)
    return hi, mid, lo


def _ada_kernel(c_ref, w_ref, b_ref, o_ref):
    cb = _silu(c_ref[...]).astype(BF16)
    o_ref[...] = jnp.dot(cb, w_ref[...].astype(BF16), preferred_element_type=F32) + b_ref[...]


def ada_modulation(cond, ada_w, ada_b):
    depth, d, n = ada_w.shape
    r = cond.shape[0]
    tn = 1024
    return pl.pallas_call(
        _ada_kernel,
        out_shape=jax.ShapeDtypeStruct((depth, r, n), F32),
        grid=(depth, n // tn),
        in_specs=[pl.BlockSpec((r, d), lambda l, j: (0, 0)),
                  pl.BlockSpec((None, d, tn), lambda l, j: (l, 0, j)),
                  pl.BlockSpec((None, 1, tn), lambda l, j: (l, 0, j))],
        out_specs=pl.BlockSpec((None, r, tn), lambda l, j: (l, 0, j)),
        compiler_params=_cparams(("arbitrary", "arbitrary")),
        name="ada_modulation",
    )(cond, ada_w, ada_b.reshape(depth, 1, n))


def _modulated_norm(x, nw, shift, scale):
    y = x * lax.rsqrt(jnp.mean(x * x, axis=-1, keepdims=True) + NORM_EPS)
    return (y * nw) * (1.0 + scale) + shift


def _head_rmsnorm(acc, w_row, eblk):
    sq = (acc * acc).astype(BF16)
    half = eblk.shape[0]
    parts = [jnp.dot(sq[:, j * half:(j + 1) * half], eblk, preferred_element_type=F32)
             for j in range(NA_WIDTH // half)]
    ms = jnp.concatenate(parts, axis=-1) * (1.0 / NA_HEAD_DIM)
    return (acc * lax.rsqrt(ms + NORM_EPS)) * w_row


def _softplus(x):
    return jnp.maximum(x, 0.0) + jnp.log(1.0 + jnp.exp(-jnp.abs(x)))


def _proj_kernel(x_ref, mod_ref, nw_ref, wq_ref, wk_ref, wv_ref, wz_ref, wxbc_ref, wdt_ref, wu_ref,
                 qn_ref, kn_ref, dtb_ref, eblk_ref,
                 q_out, k_out, v_out, kb_out, vb_out, z_out, xbc_out, dt_out, u_out):
    h = _modulated_norm(x_ref[...], nw_ref[...], mod_ref[0:1, :], mod_ref[1:2, :])
    hb = h.astype(BF16)
    dot = lambda w_ref: jnp.dot(hb, w_ref[...], preferred_element_type=F32)
    eblk = eblk_ref[...]
    q = _head_rmsnorm(dot(wq_ref), qn_ref[...], eblk)
    q_out[...] = (q * NA_SCALE).astype(BF16)
    k = _head_rmsnorm(dot(wk_ref), kn_ref[...], eblk)
    k_out[...] = k
    kb_out[...] = k.astype(BF16)
    v = dot(wv_ref)
    v_out[...] = v
    vb_out[...] = v.astype(BF16)
    z_out[...] = dot(wz_ref).astype(BF16)
    xbc_out[...] = dot(wxbc_ref).astype(BF16)
    dt_out[...] = _softplus(dot(wdt_ref) + dtb_ref[...])
    u_out[...] = dot(wu_ref).astype(BF16)


def input_projection(x, mod, nw, w_in, qn, kn, dt_bias, n_ctx_tok, lat_len, tm=512):
    t, d = x.shape
    wb = w_in.astype(BF16)
    wq, wk, wv = wb[:, COL_Q:COL_K], wb[:, COL_K:COL_V], wb[:, COL_V:COL_Z]
    wz, wxbc, wu = wb[:, COL_Z:COL_XBC], wb[:, COL_XBC:COL_DT], wb[:, COL_POOL:]
    wdt = jnp.pad(wb[:, COL_DT:COL_POOL], ((0, 0), (0, LANES - 2 * SSM_HEADS)))
    dtb = jnp.pad(dt_bias.reshape(1, 2 * SSM_HEADS), ((0, 0), (0, LANES - 2 * SSM_HEADS)))
    qn_row = jnp.tile(qn, NA_HEADS).reshape(1, NA_WIDTH)
    kn_row = jnp.tile(kn, NA_HEADS).reshape(1, NA_WIDTH)
    half = 256
    eblk = jnp.asarray(np.kron(np.eye(half // NA_HEAD_DIM), np.ones((NA_HEAD_DIM, NA_HEAD_DIM))), BF16)
    row = lambda i: (i, 0)
    tok = lambda w, dt_: (pl.BlockSpec((tm, w), row), jax.ShapeDtypeStruct((t, w), dt_))
    outs = [tok(NA_WIDTH, BF16), tok(NA_WIDTH, F32), tok(NA_WIDTH, F32), tok(NA_WIDTH, BF16),
            tok(NA_WIDTH, BF16), tok(SSM_INNER, BF16), tok(SSM_CONV_CH, BF16), tok(LANES, F32),
            tok(POOL_WIDTH, BF16)]
    consts = [nw.reshape(1, d), wq, wk, wv, wz, wxbc, wdt, wu, qn_row, kn_row, dtb, eblk]
    return pl.pallas_call(
        _proj_kernel,
        out_shape=[o[1] for o in outs],
        grid=(t // tm,),
        in_specs=[pl.BlockSpec((tm, d), row),
                  pl.BlockSpec((None, 6, d), lambda i: (_mod_row(i, tm, n_ctx_tok, lat_len), 0, 0))]
                 + [_const_spec(c.shape) for c in consts],
        out_specs=[o[0] for o in outs],
        compiler_params=_cparams(("parallel",)),
        name="input_projection",
    )(x, mod, *consts)


def _gate_kernel(x_ref, mod_ref, nw_ref, wg_ref, bg_ref, g_out):
    h = _modulated_norm(x_ref[...], nw_ref[...], mod_ref[0:1, :], mod_ref[1:2, :])
    hb = h.astype(BF16)
    d = x_ref.shape[1]
    for j in range(N_BRANCH):
        sl = slice(j * d, (j + 1) * d)
        acc = jnp.dot(hb, wg_ref[:, sl], preferred_element_type=F32) + bg_ref[:, sl]
        g_out[:, sl] = jax.nn.sigmoid(acc).astype(BF16)


def branch_gates(x, mod, nw, w_gate, b_gate, n_ctx_tok, lat_len, tm=512):
    t, d = x.shape
    n = w_gate.shape[1]
    consts = [nw.reshape(1, d), w_gate.astype(BF16), b_gate.reshape(1, n)]
    return pl.pallas_call(
        _gate_kernel,
        out_shape=jax.ShapeDtypeStruct((t, n), BF16),
        grid=(t // tm,),
        in_specs=[pl.BlockSpec((tm, d), lambda i: (i, 0)),
                  pl.BlockSpec((None, 6, d), lambda i: (_mod_row(i, tm, n_ctx_tok, lat_len), 0, 0))]
                 + [_const_spec(c.shape) for c in consts],
        out_specs=pl.BlockSpec((tm, n), lambda i: (i, 0)),
        compiler_params=_cparams(("parallel",)),
        name="branch_gates",
    )(x, mod, *consts)


def _nt_dot(a, b):
    return lax.dot_general(a, b, (((1,), (1,)), ((), ())), preferred_element_type=F32)


def _head_masks(dtype):
    lane = lax.broadcasted_iota(jnp.int32, (1, LANES), 1)
    return [((lane >= j * NA_HEAD_DIM) & (lane < (j + 1) * NA_HEAD_DIM)).astype(dtype)
            for j in range(LANES // NA_HEAD_DIM)]


def _ctx_attn_kernel(q_ref, k_ref, v_ref, o_ref):
    q, k, v = q_ref[...], k_ref[...], v_ref[...]
    out = jnp.zeros(o_ref.shape, F32)
    for hm in _head_masks(BF16):
        s = _nt_dot(q * hm, k)
        p = jnp.exp(s - jnp.max(s, axis=-1, keepdims=True))
        den = jnp.sum(p, axis=-1, keepdims=True)
        out = out + jnp.dot(p.astype(BF16), v * hm, preferred_element_type=F32) / den
    o_ref[...] = out.astype(BF16)


def context_attention(q, kb, vb, n_seq, seq_len):
    blk = lambda: pl.BlockSpec((seq_len, LANES), lambda b, hp: (b, hp))
    return pl.pallas_call(
        _ctx_attn_kernel,
        out_shape=jax.ShapeDtypeStruct((n_seq * seq_len, NA_WIDTH), BF16),
        grid=(n_seq, NA_WIDTH // LANES),
        in_specs=[blk(), blk(), blk()],
        out_specs=blk(),
        compiler_params=_cparams(("parallel", "parallel")),
        name="context_attention",
    )(q, kb, vb)


def _lat_attn_kernel(q_ref, k_ref, v_ref, kc_ref, vc_ref, bias_ref, o_ref):
    rb = pl.program_id(1)
    n_rows = k_ref.shape[0] // GRID_W
    ks = jnp.clip(rb * LAT_Q_ROWS - NA_KH // 2, 0, n_rows - LAT_K_ROWS) * GRID_W
    ks = pl.multiple_of(ks, GRID_W)
    kw = k_ref[pl.ds(ks, LAT_K_ROWS * GRID_W), :]
    vw = v_ref[pl.ds(ks, LAT_K_ROWS * GRID_W), :]
    kc = kc_ref[...].astype(BF16)
    vc = vc_ref[...].astype(BF16)
    q = q_ref[...]
    out = jnp.zeros(o_ref.shape, F32)
    for j, hm in enumerate(_head_masks(BF16)):
        qh = q * hm
        s1 = _nt_dot(qh, kw) + bias_ref[j].astype(F32)
        s2 = _nt_dot(qh, kc)
        m = jnp.maximum(jnp.max(s1, axis=-1, keepdims=True), jnp.max(s2, axis=-1, keepdims=True))
        p1 = jnp.exp(s1 - m)
        p2 = jnp.exp(s2 - m)
        den = jnp.sum(p1, axis=-1, keepdims=True) + jnp.sum(p2, axis=-1, keepdims=True)
        o = (jnp.dot(p1.astype(BF16), vw * hm, preferred_element_type=F32)
             + jnp.dot(p2.astype(BF16), vc * hm, preferred_element_type=F32))
        out = out + o / den
    o_ref[...] = out.astype(BF16)


def _latent_bias_table(rpb, rows):
    col = np.arange(GRID_W)
    cs = np.clip(col - NA_KW // 2, 0, GRID_W - NA_KW)
    kc = np.arange(GRID_W)
    valid_col = (kc[None, :] >= cs[:, None]) & (kc[None, :] < cs[:, None] + NA_KW)
    dcol = np.clip(kc[None, :] - col[:, None] + NA_KW - 1, 0, 2 * NA_KW - 2)
    toe = jnp.where(valid_col[None, None], rpb[:, :, dcol], NEG)
    toe = jnp.concatenate([toe, jnp.full_like(toe[:, :1], NEG)], axis=1)
    n_d = 2 * NA_KH - 1
    tile_idx = np.full((3, LAT_Q_ROWS, LAT_K_ROWS), n_d, np.int32)
    for ty, r0 in enumerate((0, LAT_Q_ROWS, rows - LAT_Q_ROWS)):
        ks = int(np.clip(r0 - NA_KH // 2, 0, rows - LAT_K_ROWS))
        for qr in range(LAT_Q_ROWS):
            r = r0 + qr
            rs = int(np.clip(r - NA_KH // 2, 0, rows - NA_KH))
            for kr in range(LAT_K_ROWS):
                if rs <= ks + kr < rs + NA_KH:
                    tile_idx[ty, qr, kr] = ks + kr - r + NA_KH - 1
    t = jnp.take(toe, jnp.asarray(tile_idx.reshape(-1)), axis=1)
    t = t.reshape(NA_HEADS, 3, LAT_Q_ROWS, LAT_K_ROWS, GRID_W, GRID_W)
    t = jnp.transpose(t, (1, 0, 2, 4, 3, 5))
    return t.reshape(3, NA_HEADS, LAT_Q_ROWS * GRID_W, LAT_K_ROWS * GRID_W).astype(BF16)


def latent_attention(q, kb, vb, k_ctx, v_ctx, rpb, n_ctx_tok, n_seq, seq_len):
    rows = seq_len // GRID_W
    n_rb = rows // LAT_Q_ROWS
    tq = LAT_Q_ROWS * GRID_W
    lc = k_ctx.shape[1]
    bias = _latent_bias_table(rpb, rows)
    hp_n = NA_WIDTH // LANES
    heads_per = LANES // NA_HEAD_DIM
    q0 = n_ctx_tok // tq
    s0 = n_ctx_tok // seq_len
    btype = lambda rb: jnp.where(rb == 0, 0, jnp.where(rb == n_rb - 1, 2, 1))
    return pl.pallas_call(
        _lat_attn_kernel,
        out_shape=jax.ShapeDtypeStruct((n_seq * seq_len, NA_WIDTH), BF16),
        grid=(hp_n, n_rb, n_seq),
        in_specs=[pl.BlockSpec((tq, LANES), lambda hp, rb, b: (q0 + b * n_rb + rb, hp)),
                  pl.BlockSpec((seq_len, LANES), lambda hp, rb, b: (s0 + b, hp)),
                  pl.BlockSpec((seq_len, LANES), lambda hp, rb, b: (s0 + b, hp)),
                  pl.BlockSpec((None, lc, LANES), lambda hp, rb, b: (b, 0, hp)),
                  pl.BlockSpec((None, lc, LANES), lambda hp, rb, b: (b, 0, hp)),
                  pl.BlockSpec((None, heads_per, tq, LAT_K_ROWS * GRID_W),
                               lambda hp, rb, b: (btype(rb), hp, 0, 0))],
        out_specs=pl.BlockSpec((tq, LANES), lambda hp, rb, b: (b * n_rb + rb, hp)),
        compiler_params=_cparams(("arbitrary", "arbitrary", "arbitrary")),
        name="latent_attention",
    )(q, kb, vb, k_ctx, v_ctx, bias)


def _local_kernel(seq_tiles_ref, xc_ref, xp_ref, xn_ref, uc_ref, up_ref, un_ref, cw_ref, cb_ref,
                  bandc_ref, bandp_ref, bandn_ref, pw_ref, ps_ref, xbc_out, p_out, ext_ref):
    i = pl.program_id(0)
    tl = xc_ref.shape[0]
    j = seq_tiles_ref[0, i]
    nt = seq_tiles_ref[1, i]
    has_prev = (j > 0).astype(F32)
    has_next = (j < nt - 1).astype(F32)

    ext_ref[0:HALO, :] = xp_ref[...].astype(F32) * has_prev
    ext_ref[HALO:HALO + tl, :] = xc_ref[...].astype(F32)
    ext_ref[HALO + tl:, :] = xn_ref[...].astype(F32) * has_next
    pad = SSM_CONV // 2
    for lt in range(xc_ref.shape[1] // LANES):
        ls = slice(lt * LANES, (lt + 1) * LANES)
        taps = [cw_ref[k:k + 1, ls] for k in range(SSM_CONV)]
        bias = cb_ref[:, ls]
        for rb in range(tl // CONV_ROWS):
            r0 = HALO + rb * CONV_ROWS - pad
            acc = bias + ext_ref[r0:r0 + CONV_ROWS, ls] * taps[0]
            for k in range(1, SSM_CONV):
                acc = acc + ext_ref[r0 + k:r0 + k + CONV_ROWS, ls] * taps[k]
            xbc_out[rb * CONV_ROWS:(rb + 1) * CONV_ROWS, ls] = _silu(acc).astype(BF16)

    t = j * tl + lax.broadcasted_iota(jnp.int32, (tl, 1), 0)
    seq_len = nt * tl
    uprev = (up_ref[...].astype(F32) * has_prev).astype(BF16)
    unext = (un_ref[...].astype(F32) * has_next).astype(BF16)
    for g, w in enumerate(POOL_WINDOWS):
        sl = slice(g * POOL_GROUP, (g + 1) * POOL_GROUP)
        ucur = uc_ref[:, sl]
        ssum = (jnp.dot(bandc_ref[g], ucur, preferred_element_type=F32)
                + jnp.dot(bandp_ref[g], uprev[:, sl], preferred_element_type=F32)
                + jnp.dot(bandn_ref[g], unext[:, sl], preferred_element_type=F32))
        lo = jnp.clip(t - w // 2, 0, seq_len)
        hi = jnp.clip(t - w // 2 + w, 0, seq_len)
        pooled = ssum / (hi - lo).astype(F32) - ucur.astype(F32)
        mixed = jnp.dot(pooled.astype(BF16), pw_ref[g], preferred_element_type=F32)
        p_out[:, sl] = (mixed * ps_ref[:, sl]).astype(BF16)


def _pool_bands(tl):
    t = np.arange(tl)[:, None]
    bc = np.zeros((len(POOL_WINDOWS), tl, tl), np.float32)
    bp = np.zeros((len(POOL_WINDOWS), tl, HALO), np.float32)
    bn = np.zeros((len(POOL_WINDOWS), tl, HALO), np.float32)
    for g, w in enumerate(POOL_WINDOWS):
        lo, hi = t - w // 2, t - w // 2 + w
        jc = np.arange(tl)[None, :]
        bc[g] = (jc >= lo) & (jc < hi)
        jp = np.arange(-HALO, 0)[None, :]
        bp[g] = (jp >= lo) & (jp < hi)
        jn = np.arange(tl, tl + HALO)[None, :]
        bn[g] = (jn >= lo) & (jn < hi)
    return jnp.asarray(bc, BF16), jnp.asarray(bp, BF16), jnp.asarray(bn, BF16)


def _seq_tile_table(seq_lens, tile):
    pos, cnt = [], []
    for sl in seq_lens:
        n = sl // tile
        pos += list(range(n))
        cnt += [n] * n
    return np.asarray([pos, cnt], np.int32)


def local_mixers(xbc, u, conv_w, conv_b, pool_w, pool_scale, seq_lens):
    t, c = xbc.shape
    tl = LOCAL_TILE
    table = jnp.asarray(_seq_tile_table(seq_lens, tl))
    hb = tl // HALO
    n_halo = t // HALO
    cur = lambda w: pl.BlockSpec((tl, w), lambda i, s: (i, 0))
    prev = lambda w: pl.BlockSpec((HALO, w), lambda i, s: (jnp.maximum(i * hb - 1, 0), 0))
    nxt = lambda w: pl.BlockSpec((HALO, w), lambda i, s: (jnp.minimum((i + 1) * hb, n_halo - 1), 0))
    cst = lambda a: pl.BlockSpec(a.shape, lambda i, s: (0,) * a.ndim, pipeline_mode=pl.Buffered(1))
    bc, bp, bn = _pool_bands(tl)
    consts = [conv_w, conv_b.reshape(1, c), bc, bp, bn, pool_w.astype(BF16), pool_scale.reshape(1, POOL_WIDTH)]
    return pl.pallas_call(
        _local_kernel,
        out_shape=[jax.ShapeDtypeStruct((t, c), BF16), jax.ShapeDtypeStruct((t, POOL_WIDTH), BF16)],
        grid_spec=pltpu.PrefetchScalarGridSpec(
            num_scalar_prefetch=1,
            grid=(t // tl,),
            in_specs=[cur(c), prev(c), nxt(c), cur(POOL_WIDTH), prev(POOL_WIDTH), nxt(POOL_WIDTH)]
                     + [cst(a) for a in consts],
            out_specs=[cur(c), cur(POOL_WIDTH)],
            scratch_shapes=[pltpu.VMEM((tl + 2 * HALO, c), F32)]),
        compiler_params=_cparams(("arbitrary",)),
        name="local_mixers",
    )(table, xbc, xbc, xbc, u, u, u, *consts)


def _ssd_direction(xbc_ref, dt_ref, alog_ref, expand_ref, state_ref, y_out, reverse):
    cl = xbc_ref.shape[0]
    gw = SSM_INNER // SSM_GROUPS
    lane0 = SSM_HEADS if reverse else 0
    xs = xbc_ref[:, :SSM_INNER].astype(F32)
    bm = xbc_ref[:, SSM_INNER:SSM_INNER + SSM_GROUPS * SSM_STATE]
    cm = xbc_ref[:, SSM_INNER + SSM_GROUPS * SSM_STATE:]
    dt = dt_ref[...]
    a = dt * (-jnp.exp(alog_ref[...]))
    ri = lax.broadcasted_iota(jnp.int32, (cl, cl), 0)
    ci = lax.broadcasted_iota(jnp.int32, (cl, cl), 1)
    tri = (ci >= ri) if reverse else (ci <= ri)
    trib = tri.astype(BF16)
    acum = sum(jnp.dot(trib, part, preferred_element_type=F32) for part in _split3(a))
    acum_t = acum.T
    last = 0 if reverse else cl - 1
    atot = acum[last:last + 1, :]
    stacked = jnp.concatenate([dt, jnp.exp(acum), jnp.exp(atot - acum)], axis=0)
    ex = sum(jnp.dot(part, expand_ref[...], preferred_element_type=F32) for part in _split2(stacked))
    dt_e, eacum_e, wend_e = ex[:cl], ex[cl:2 * cl], ex[2 * cl:]
    xdt = xs * dt_e
    xdt_b = xdt.astype(BF16)
    lane = lax.broadcasted_iota(jnp.int32, (1, LANES), 1)
    left = lane < SSM_HEAD_DIM
    state = state_ref[...]
    state_b = state.astype(BF16)
    y_parts = []
    for g in range(SSM_GROUPS):
        bg = bm[:, g * SSM_STATE:(g + 1) * SSM_STATE]
        cg = cm[:, g * SSM_STATE:(g + 1) * SSM_STATE]
        cb = _nt_dot(cg, bg)
        y_off = jnp.dot(cg, state_b[:, g * gw:(g + 1) * gw], preferred_element_type=F32)
        heads_g = SSM_HEADS // SSM_GROUPS
        diag = []
        for pair in range(heads_g // 2):
            ms = []
            for hh in range(2):
                hl = lane0 + g * heads_g + pair * 2 + hh
                seg = acum[:, hl:hl + 1] - acum_t[hl:hl + 1, :]
                ms.append((cb * jnp.where(tri, jnp.exp(jnp.minimum(seg, 0.0)), 0.0)).astype(BF16))
            c0 = g * gw + pair * LANES
            xp = xdt_b[:, c0:c0 + LANES]
            rhs = jnp.concatenate([jnp.where(left, xp, jnp.zeros_like(xp)),
                                   jnp.where(left, jnp.zeros_like(xp), xp)], axis=0)
            diag.append(jnp.dot(jnp.concatenate(ms, axis=1), rhs, preferred_element_type=F32))
        y_parts.append(jnp.concatenate(diag, axis=1) + y_off * eacum_e[:, g * gw:(g + 1) * gw])
        xw = (xdt[:, g * gw:(g + 1) * gw] * wend_e[:, g * gw:(g + 1) * gw]).astype(BF16)
        upd = jnp.dot(bg.astype(F32).T.astype(BF16), xw, preferred_element_type=F32)
        state_ref[:, g * gw:(g + 1) * gw] = (state[:, g * gw:(g + 1) * gw]
                                             * eacum_e[last:last + 1, g * gw:(g + 1) * gw] + upd)
    y_out[...] = jnp.concatenate(y_parts, axis=1)


def _ssd_kernel(tab_ref, xf_ref, xb_ref, dtf_ref, dtb_ref, alog_ref, ef_ref, eb_ref, h0f_ref, h0b_ref,
                yf_out, yb_out, sf_out, sb_out, state_f, state_b):
    g = pl.program_id(0)

    @pl.when(tab_ref[1, g] == 1)
    def _():
        state_f[...] = h0f_ref[...].T
        state_b[...] = h0b_ref[...].T

    _ssd_direction(xf_ref, dtf_ref, alog_ref, ef_ref, state_f, yf_out, reverse=False)
    _ssd_direction(xb_ref, dtb_ref, alog_ref, eb_ref, state_b, yb_out, reverse=True)

    @pl.when(tab_ref[2, g] == 1)
    def _():
        sf_out[...] = state_f[...].T
        sb_out[...] = state_b[...].T


def _ssd_tables(seq_lens, cl):
    bwd, first, last, seq = [], [], [], []
    start = 0
    for s, sl in enumerate(seq_lens):
        nc = sl // cl
        for c in range(nc):
            bwd.append(start + nc - 1 - c)
            first.append(int(c == 0))
            last.append(int(c == nc - 1))
            seq.append(s)
        start += nc
    return np.asarray([bwd, first, last, seq], np.int32)


def ssd_scan(xbc, dt, a_log, h0_f, h0_b, seq_lens):
    t = xbc.shape[0]
    cl = SSD_CHUNK
    n_seq = len(seq_lens)
    tab = jnp.asarray(_ssd_tables(seq_lens, cl))
    alog = jnp.pad(a_log.reshape(1, 2 * SSM_HEADS), ((0, 0), (0, LANES - 2 * SSM_HEADS)))
    ef = np.zeros((LANES, SSM_INNER), np.float32)
    eb = np.zeros((LANES, SSM_INNER), np.float32)
    for h in range(SSM_HEADS):
        ef[h, h * SSM_HEAD_DIM:(h + 1) * SSM_HEAD_DIM] = 1.0
        eb[SSM_HEADS + h, h * SSM_HEAD_DIM:(h + 1) * SSM_HEAD_DIM] = 1.0
    fwd = lambda w: pl.BlockSpec((cl, w), lambda g, tb: (g, 0))
    bwd = lambda w: pl.BlockSpec((cl, w), lambda g, tb: (tb[0, g], 0))
    cst = lambda shape: pl.BlockSpec(shape, lambda g, tb: (0,) * len(shape), pipeline_mode=pl.Buffered(1))
    st = lambda: pl.BlockSpec((None, SSM_INNER, SSM_STATE), lambda g, tb: (tb[3, g], 0, 0))
    return pl.pallas_call(
        _ssd_kernel,
        out_shape=[jax.ShapeDtypeStruct((t, SSM_INNER), F32), jax.ShapeDtypeStruct((t, SSM_INNER), F32),
                   jax.ShapeDtypeStruct((n_seq, SSM_INNER, SSM_STATE), F32),
                   jax.ShapeDtypeStruct((n_seq, SSM_INNER, SSM_STATE), F32)],
        grid_spec=pltpu.PrefetchScalarGridSpec(
            num_scalar_prefetch=1,
            grid=(t // cl,),
            in_specs=[fwd(SSM_CONV_CH), bwd(SSM_CONV_CH), fwd(LANES), bwd(LANES),
                      cst((1, LANES)), cst((LANES, SSM_INNER)), cst((LANES, SSM_INNER)), st(), st()],
            out_specs=[fwd(SSM_INNER), bwd(SSM_INNER), st(), st()],
            scratch_shapes=[pltpu.VMEM((SSM_STATE, SSM_INNER), F32), pltpu.VMEM((SSM_STATE, SSM_INNER), F32)]),
        compiler_params=_cparams(("arbitrary",)),
        name="ssd_scan",
    )(tab, xbc, xbc, dt, dt, alog, jnp.asarray(ef, BF16), jnp.asarray(eb, BF16), h0_f, h0_b)


def _merge_kernel(x_ref, mod_ref, a_ref, yf_ref, yb_ref, xs_ref, z_ref, p_ref, g_ref,
                  dsum_ref, snw_ref, wna_ref, wssm_ref, wpool_ref, wout_ref, n2_ref, *rest,
                  with_router):
    if with_router:
        router_ref, x1_out, h2_out, logit_out = rest
    else:
        x1_out, h2_out = rest
    d = x_ref.shape[1]
    y = yf_ref[...] + yb_ref[...] + dsum_ref[...] * xs_ref[...].astype(F32)
    y = y * _silu(z_ref[...].astype(F32))
    gw = SSM_INNER // SSM_GROUPS
    parts = []
    for g in range(SSM_GROUPS):
        yg = y[:, g * gw:(g + 1) * gw]
        yg = yg * lax.rsqrt(jnp.mean(yg * yg, axis=-1, keepdims=True) + NORM_EPS)
        parts.append(yg * snw_ref[:, g * gw:(g + 1) * gw])
    s = jnp.concatenate(parts, axis=1).astype(BF16)
    gate = lambda j: g_ref[:, j * d:(j + 1) * d].astype(F32)
    merged = (gate(0) * jnp.dot(a_ref[...], wna_ref[...], preferred_element_type=F32)
              + gate(1) * jnp.dot(s, wssm_ref[...], preferred_element_type=F32)
              + gate(2) * jnp.dot(p_ref[...], wpool_ref[...], preferred_element_type=F32))
    x1 = x_ref[...] + mod_ref[2:3, :] * jnp.dot(merged.astype(BF16), wout_ref[...],
                                                preferred_element_type=F32)
    x1_out[...] = x1
    h2 = _modulated_norm(x1, n2_ref[...], mod_ref[3:4, :], mod_ref[4:5, :])
    if with_router:
        _store_token_major(h2_out, h2)
        hh, hl = _split2(h2)
        rh, rl = router_ref[0], router_ref[1]
        logit_out[...] = (jnp.dot(hh, rh, preferred_element_type=F32)
                          + jnp.dot(hl, rh, preferred_element_type=F32)
                          + jnp.dot(hh, rl, preferred_element_type=F32))
    else:
        h2_out[...] = h2.astype(BF16)


def merge_branches(x, mod, a, yf, yb, xbc, z, p, gates, d_sum, ssm_norm_w, wb_na, wb_ssm, wb_pool, w_out,
                   norm2_w, router, n_ctx_tok, lat_len, tm=512):
    t, d = x.shape
    row = lambda i: (i, 0)
    consts = [jnp.repeat(d_sum, SSM_HEAD_DIM).reshape(1, SSM_INNER), ssm_norm_w.reshape(1, SSM_INNER),
              wb_na.astype(BF16), wb_ssm.astype(BF16), wb_pool.astype(BF16), w_out.astype(BF16),
              norm2_w.reshape(1, d)]
    out_shape = [jax.ShapeDtypeStruct((t, d), F32)]
    out_specs = [pl.BlockSpec((tm, d), row)]
    with_router = router is not None
    if with_router:
        rpad = jnp.pad(router, ((0, 0), (0, LANES - router.shape[1])))
        rh = rpad.astype(BF16)
        rl = (rpad - rh.astype(F32)).astype(BF16)
        consts.append(jnp.stack([rh, rl]))
        out_shape += [jax.ShapeDtypeStruct((t * ROW_TILES, LANES), F32), jax.ShapeDtypeStruct((t, LANES), F32)]
        out_specs += [pl.BlockSpec((tm * ROW_TILES, LANES), row), pl.BlockSpec((tm, LANES), row)]
    else:
        out_shape += [jax.ShapeDtypeStruct((t, d), BF16)]
        out_specs += [pl.BlockSpec((tm, d), row)]
    return pl.pallas_call(
        functools.partial(_merge_kernel, with_router=with_router),
        out_shape=out_shape,
        grid=(t // tm,),
        in_specs=[pl.BlockSpec((tm, d), row),
                  pl.BlockSpec((None, 6, d), lambda i: (_mod_row(i, tm, n_ctx_tok, lat_len), 0, 0)),
                  pl.BlockSpec((tm, NA_WIDTH), row), pl.BlockSpec((tm, SSM_INNER), row),
                  pl.BlockSpec((tm, SSM_INNER), row), pl.BlockSpec((tm, SSM_INNER), row),
                  pl.BlockSpec((tm, SSM_INNER), row), pl.BlockSpec((tm, POOL_WIDTH), row),
                  pl.BlockSpec((tm, N_BRANCH * d), row)]
                 + [_const_spec(c.shape) for c in consts],
        out_specs=out_specs,
        compiler_params=_cparams(("parallel",)),
        name="merge_branches",
    )(x, mod, a, yf, yb, xbc, z, p, gates, *consts)


def _ffn_kernel(h_ref, x_ref, mod_ref, w1_ref, w3_ref, w2_ref, o_ref, acc_ref, *, tf):
    f = w1_ref.shape[1]
    hb = h_ref[...]
    for c in range(f // tf):
        sl = slice(c * tf, (c + 1) * tf)
        u = (_silu(jnp.dot(hb, w1_ref[:, sl], preferred_element_type=F32))
             * jnp.dot(hb, w3_ref[:, sl], preferred_element_type=F32))
        part = jnp.dot(u.astype(BF16), w2_ref[sl, :], preferred_element_type=F32)
        if c == 0:
            acc_ref[...] = part
        else:
            acc_ref[...] += part
    o_ref[...] = x_ref[...] + mod_ref[5:6, :] * acc_ref[...]


def dense_ffn(h2, x1, mod, w1, w3, w2, n_ctx_tok, lat_len, tm=512, tf=256):
    t, d = x1.shape
    consts = [w1.astype(BF16), w3.astype(BF16), w2.astype(BF16)]
    return pl.pallas_call(
        functools.partial(_ffn_kernel, tf=tf),
        out_shape=jax.ShapeDtypeStruct((t, d), F32),
        grid=(t // tm,),
        in_specs=[pl.BlockSpec((tm, d), lambda i: (i, 0)), pl.BlockSpec((tm, d), lambda i: (i, 0)),
                  pl.BlockSpec((None, 6, d), lambda i: (_mod_row(i, tm, n_ctx_tok, lat_len), 0, 0))]
                 + [_const_spec(c.shape) for c in consts],
        out_specs=pl.BlockSpec((tm, d), lambda i: (i, 0)),
        scratch_shapes=[pltpu.VMEM((tm, d), F32)],
        compiler_params=_cparams(("parallel",)),
        name="dense_ffn",
    )(h2, x1, mod, *consts)


def _route_kernel(logit_ref, info_out, gate_out, count_out, base_ref):
    i = pl.program_id(0)

    @pl.when(i == 0)
    def _():
        base_ref[...] = jnp.zeros_like(base_ref)

    lg = logit_ref[...]
    tm = lg.shape[0]
    lane = lax.broadcasted_iota(jnp.int32, lg.shape, 1)
    valid = lane < N_EXPERTS
    lg = jnp.where(valid, lg, -jnp.inf)
    m1 = jnp.max(lg, axis=-1, keepdims=True)
    e1 = jnp.min(jnp.where(lg == m1, lane, LANES), axis=-1, keepdims=True)
    lg2 = jnp.where(lane == e1, -jnp.inf, lg)
    m2 = jnp.max(lg2, axis=-1, keepdims=True)
    e2 = jnp.min(jnp.where(lg2 == m2, lane, LANES), axis=-1, keepdims=True)
    g1 = 1.0 / (1.0 + jnp.exp(m2 - m1))
    oh1 = (lane == e1).astype(F32)
    oh2 = (lane == e2).astype(F32)
    oh = oh1 + oh2
    ri = lax.broadcasted_iota(jnp.int32, (tm, tm), 0)
    ci = lax.broadcasted_iota(jnp.int32, (tm, tm), 1)
    before = jnp.dot((ci < ri).astype(BF16), oh.astype(BF16), preferred_element_type=F32) + base_ref[...]
    r1 = jnp.sum(oh1 * before, axis=-1, keepdims=True)
    r2 = jnp.sum(oh2 * before, axis=-1, keepdims=True)
    lane8 = lax.broadcasted_iota(jnp.int32, (tm, 8), 1)
    info = jnp.where(lane8 == 0, e1, jnp.where(lane8 == 1, e2,
                     jnp.where(lane8 == 2, r1.astype(jnp.int32), r2.astype(jnp.int32))))
    info_out[...] = info
    gate_out[...] = jnp.where(lane8 == 0, g1, 1.0 - g1)
    base_ref[...] += jnp.sum(oh, axis=0, keepdims=True)
    count_out[...] = base_ref[...].astype(jnp.int32)


def route_tokens(logits, tm=512):
    t = logits.shape[0]
    return pl.pallas_call(
        _route_kernel,
        out_shape=[jax.ShapeDtypeStruct((t, 8), jnp.int32), jax.ShapeDtypeStruct((t, 8), F32),
                   jax.ShapeDtypeStruct((1, LANES), jnp.int32)],
        grid=(t // tm,),
        in_specs=[pl.BlockSpec((tm, LANES), lambda i: (i, 0))],
        out_specs=[pl.BlockSpec((tm, 8), lambda i: (i, 0)), pl.BlockSpec((tm, 8), lambda i: (i, 0)),
                   pl.BlockSpec((1, LANES), lambda i: (0, 0))],
        scratch_shapes=[pltpu.VMEM((1, LANES), F32)],
        compiler_params=_cparams(("arbitrary",)),
        name="route_tokens",
    )(logits)


def _row_dma_loops(copy, n):
    def start(i, c):
        copy(i).start()
        return c

    def wait(i, c):
        copy(i).wait()
        return c

    lax.fori_loop(0, n, start, 0, unroll=8)
    lax.fori_loop(0, n, wait, 0, unroll=8)


def _token_rows(idx):
    return pl.ds(pl.multiple_of(idx * ROW_TILES, ROW_TILES), ROW_TILES)


def _dispatch_kernel(dest_ref, h_ref, xb_in, xb_out, sem):
    del xb_in
    tm = h_ref.shape[0] // ROW_TILES

    def copy(n):
        return pltpu.make_async_copy(h_ref.at[_token_rows(n // 2)], xb_out.at[_token_rows(dest_ref[0, n])], sem)

    _row_dma_loops(copy, 2 * tm)


def dispatch_rows(h2, dest, n_slots, tm=256):
    t = h2.shape[0] // ROW_TILES
    dest_tiles = dest.reshape(t // tm, 1, 2 * tm)
    xb0 = jnp.zeros((n_slots * ROW_TILES, LANES), h2.dtype)
    return pl.pallas_call(
        _dispatch_kernel,
        out_shape=jax.ShapeDtypeStruct(xb0.shape, h2.dtype),
        grid=(t // tm,),
        in_specs=[pl.BlockSpec((None, 1, 2 * tm), lambda i: (i, 0, 0), memory_space=pltpu.SMEM),
                  pl.BlockSpec((tm * ROW_TILES, LANES), lambda i: (i, 0)),
                  pl.BlockSpec(memory_space=pl.ANY)],
        out_specs=pl.BlockSpec(memory_space=pl.ANY),
        scratch_shapes=[pltpu.SemaphoreType.DMA(())],
        input_output_aliases={2: 0},
        compiler_params=_cparams(("arbitrary",)),
        name="moe_dispatch",
    )(dest_tiles, h2, xb0)


def _expert_kernel(ge_ref, nu_ref, x_ref, w1_ref, w3_ref, w2_ref, o_ref, xs_ref, acc_ref):
    g = pl.program_id(0)
    f = pl.program_id(1)
    tg = xs_ref.shape[0]
    used = g < nu_ref[0]

    @pl.when(used & (f == 0))
    def _():
        for k in range(ROW_TILES):
            xs_ref[:, k * LANES:(k + 1) * LANES] = _load_token_major(x_ref, tg, k).astype(BF16)

    def partial_out():
        xb = xs_ref[...]
        u = (_silu(jnp.dot(xb, w1_ref[...], preferred_element_type=F32))
             * jnp.dot(xb, w3_ref[...], preferred_element_type=F32))
        return jnp.dot(u.astype(BF16), w2_ref[...], preferred_element_type=F32)

    @pl.when(used & (f == 0))
    def _():
        acc_ref[...] = partial_out()

    @pl.when(used & (f == 1))
    def _():
        _store_token_major(o_ref, acc_ref[...] + partial_out())

    @pl.when(jnp.logical_not(used) & (f == 1))
    def _():
        o_ref[...] = jnp.zeros_like(o_ref)


def expert_ffn(xb, group_expert, n_used, w1, w3, w2):
    n_slots = xb.shape[0] // ROW_TILES
    d, ff = w1.shape[1], w1.shape[2]
    tf = ff // 2
    tg = MOE_GROUP
    fi = lambda g, f, nu: jnp.where(g < nu[0], f, 0)
    return pl.pallas_call(
        _expert_kernel,
        out_shape=jax.ShapeDtypeStruct(xb.shape, F32),
        grid_spec=pltpu.PrefetchScalarGridSpec(
            num_scalar_prefetch=2,
            grid=(n_slots // tg, 2),
            in_specs=[pl.BlockSpec((tg * ROW_TILES, LANES), lambda g, f, ge, nu: (g, 0)),
                      pl.BlockSpec((None, d, tf), lambda g, f, ge, nu: (ge[g], 0, fi(g, f, nu))),
                      pl.BlockSpec((None, d, tf), lambda g, f, ge, nu: (ge[g], 0, fi(g, f, nu))),
                      pl.BlockSpec((None, tf, d), lambda g, f, ge, nu: (ge[g], fi(g, f, nu), 0))],
            out_specs=pl.BlockSpec((tg * ROW_TILES, LANES), lambda g, f, ge, nu: (g, 0)),
            scratch_shapes=[pltpu.VMEM((tg, d), BF16), pltpu.VMEM((tg, d), F32)]),
        compiler_params=_cparams(("arbitrary", "arbitrary")),
        name="expert_ffn",
    )(group_expert, n_used, xb, w1.astype(BF16), w3.astype(BF16), w2.astype(BF16))


def _combine_kernel(dest_ref, yb_ref, x_ref, mod_ref, gate_ref, o_ref, rows_ref, sem):
    tm = x_ref.shape[0]

    def copy(n):
        return pltpu.make_async_copy(yb_ref.at[_token_rows(dest_ref[0, n])],
                                     rows_ref.at[n % 2, _token_rows(n // 2)], sem)

    _row_dma_loops(copy, 2 * tm)
    g0, g1 = gate_ref[:, 0:1], gate_ref[:, 1:2]
    for k in range(ROW_TILES):
        sl = slice(k * LANES, (k + 1) * LANES)
        y = _load_token_major(rows_ref.at[0], tm, k) * g0 + _load_token_major(rows_ref.at[1], tm, k) * g1
        o_ref[:, sl] = x_ref[:, sl] + mod_ref[5:6, sl] * y


def combine_rows(yb, dest, gates, x1, mod, n_ctx_tok, lat_len, tm=256):
    t, d = x1.shape
    dest_tiles = dest.reshape(t // tm, 1, 2 * tm)
    return pl.pallas_call(
        _combine_kernel,
        out_shape=jax.ShapeDtypeStruct((t, d), F32),
        grid=(t // tm,),
        in_specs=[pl.BlockSpec((None, 1, 2 * tm), lambda i: (i, 0, 0), memory_space=pltpu.SMEM),
                  pl.BlockSpec(memory_space=pl.ANY),
                  pl.BlockSpec((tm, d), lambda i: (i, 0)),
                  pl.BlockSpec((None, 6, d), lambda i: (_mod_row(i, tm, n_ctx_tok, lat_len), 0, 0)),
                  pl.BlockSpec((tm, 8), lambda i: (i, 0))],
        out_specs=pl.BlockSpec((tm, d), lambda i: (i, 0)),
        scratch_shapes=[pltpu.VMEM((2, tm * ROW_TILES, LANES), F32), pltpu.SemaphoreType.DMA(())],
        compiler_params=_cparams(("arbitrary",)),
        name="moe_combine",
    )(dest_tiles, yb, x1, mod, gates)


def moe_ffn(h2, logits, x1, mod, w1, w3, w2, n_ctx_tok, lat_len):
    t = x1.shape[0]
    info, gates, counts = route_tokens(logits)
    counts = counts[0, :N_EXPERTS]
    padded = (counts + MOE_GROUP - 1) // MOE_GROUP * MOE_GROUP
    pad_end = jnp.cumsum(padded)
    pad_start = pad_end - padded
    experts = info[:, 0:2]
    dest = jnp.sum(jnp.where(experts[..., None] == jnp.arange(N_EXPERTS), pad_start, 0), axis=-1) + info[:, 2:4]
    n_groups = 2 * t // MOE_GROUP + N_EXPERTS
    group_start = jnp.arange(n_groups, dtype=jnp.int32) * MOE_GROUP
    group_expert = jnp.minimum(jnp.sum(group_start[:, None] >= pad_end[None, :], axis=-1), N_EXPERTS - 1)
    dest = dest.astype(jnp.int32)
    n_used = (pad_end[-1:] // MOE_GROUP).astype(jnp.int32)
    xb = dispatch_rows(h2, dest, n_groups * MOE_GROUP)
    yb = expert_ffn(xb, group_expert.astype(jnp.int32), n_used, w1, w3, w2)
    return combine_rows(yb, dest, gates, x1, mod, n_ctx_tok, lat_len)


def kernel(x_prompt, x_sample, c, cache_na_k, cache_na_v, state_ssd_fwd, state_ssd_bwd, c_ctx, norm1_w, norm2_w, ada_w, ada_b, w_in, na_q_norm, na_k_norm, na_rpb, ssm_conv_w, ssm_conv_b, ssm_dt_bias, ssm_a_log, ssm_d, ssm_norm_w, pool_w, pool_scale, w_branch_na, w_branch_ssm, w_branch_pool, w_gate, b_gate, w_out, ffn_w1, ffn_w3, ffn_w2, moe_router, moe_w1, moe_w3, moe_w2):
    n_ctx, ctx_len, d = x_prompt.shape
    n_lat, lat_len, _ = x_sample.shape
    depth = ada_w.shape[0]
    n_ctx_tok = n_ctx * ctx_len
    seq_lens = (ctx_len,) * n_ctx + (lat_len,) * n_lat
    lc = cache_na_k.shape[2]

    x = jnp.concatenate([x_prompt.reshape(n_ctx_tok, d), x_sample.reshape(n_lat * lat_len, d)], axis=0)
    cond = jnp.concatenate([c_ctx[None, :], c], axis=0)
    cond = jnp.pad(cond, ((0, -cond.shape[0] % 8), (0, 0)))
    mods = ada_modulation(cond, ada_w, ada_b).reshape(depth, cond.shape[0], 6, d)
    zero_state = jnp.zeros((n_ctx, SSM_INNER, SSM_STATE), F32)

    ks, vs, sfs, sbs = [], [], [], []
    for i in range(depth):
        mod = mods[i]
        q, k, v, kb, vb, z, xbc, dt, u = input_projection(
            x, mod, norm1_w[i], w_in[i], na_q_norm[i], na_k_norm[i], ssm_dt_bias[i], n_ctx_tok, lat_len)
        gates = branch_gates(x, mod, norm1_w[i], w_gate[i], b_gate[i], n_ctx_tok, lat_len)
        a_ctx = context_attention(q, kb, vb, n_ctx, ctx_len)
        a_lat = latent_attention(q, kb, vb, cache_na_k[:, i].reshape(n_lat, lc, NA_WIDTH),
                                 cache_na_v[:, i].reshape(n_lat, lc, NA_WIDTH), na_rpb[i],
                                 n_ctx_tok, n_lat, lat_len)
        a = jnp.concatenate([a_ctx, a_lat], axis=0)
        xbc_c, p = local_mixers(xbc, u, ssm_conv_w[i], ssm_conv_b[i], pool_w[i], pool_scale[i], seq_lens)
        h0_f = jnp.concatenate([zero_state, state_ssd_fwd[:, i].reshape(n_lat, SSM_INNER, SSM_STATE)], axis=0)
        h0_b = jnp.concatenate([zero_state, state_ssd_bwd[:, i].reshape(n_lat, SSM_INNER, SSM_STATE)], axis=0)
        yf, yb, s_f, s_b = ssd_scan(xbc_c, dt, ssm_a_log[i], h0_f, h0_b, seq_lens)
        j = i // 2
        router = moe_router[j] if i % 2 == 1 else None
        res = merge_branches(x, mod, a, yf, yb, xbc_c, z, p, gates, ssm_d[i, 0] + ssm_d[i, 1],
                             ssm_norm_w[i], w_branch_na[i], w_branch_ssm[i], w_branch_pool[i], w_out[i],
                             norm2_w[i], router, n_ctx_tok, lat_len)
        if i % 2 == 0:
            x1, h2 = res
            x = dense_ffn(h2, x1, mod, ffn_w1[j], ffn_w3[j], ffn_w2[j], n_ctx_tok, lat_len)
        else:
            x1, h2, logits = res
            x = moe_ffn(h2, logits, x1, mod, moe_w1[j], moe_w3[j], moe_w2[j], n_ctx_tok, lat_len)
        ks.append(k[:n_ctx_tok].reshape(n_ctx, ctx_len, NA_HEADS, NA_HEAD_DIM))
        vs.append(v[:n_ctx_tok].reshape(n_ctx, ctx_len, NA_HEADS, NA_HEAD_DIM))
        sfs.append(s_f[:n_ctx].reshape(n_ctx, SSM_HEADS, SSM_HEAD_DIM, SSM_STATE))
        sbs.append(s_b[:n_ctx].reshape(n_ctx, SSM_HEADS, SSM_HEAD_DIM, SSM_STATE))

    y_prompt = x[:n_ctx_tok].reshape(n_ctx, ctx_len, d)
    y_sample = x[n_ctx_tok:].reshape(n_lat, lat_len, d)
    return (y_prompt, y_sample, jnp.stack(ks, axis=1), jnp.stack(vs, axis=1),
            jnp.stack(sfs, axis=1), jnp.stack(sbs, axis=1))
```

```python
import functools
import math

import numpy as np
import jax
import jax.numpy as jnp
from jax import lax
from jax.experimental import pallas as pl
from jax.experimental.pallas import tpu as pltpu

F32 = jnp.float32
BF16 = jnp.bfloat16

D_MODEL = 1024
GRID_W = 64
NORM_EPS = 1e-6
N_BRANCH = 3
NA_HEADS = D_MODEL // 128
NA_HEAD_DIM = 64
NA_WIDTH = NA_HEADS * NA_HEAD_DIM
NA_KH = 8
NA_KW = 16
NA_SCALE = NA_HEAD_DIM ** -0.5
SSM_INNER = D_MODEL
SSM_HEAD_DIM = 64
SSM_HEADS = SSM_INNER // SSM_HEAD_DIM
SSM_GROUPS = 2
SSM_STATE = 128
SSM_CONV = 5
SSM_CONV_CH = SSM_INNER + 2 * SSM_GROUPS * SSM_STATE
POOL_WIDTH = D_MODEL // 2
POOL_WINDOWS = (2, 4, 8, 16)
POOL_GROUP = POOL_WIDTH // len(POOL_WINDOWS)
COL_Q = 0
COL_K = COL_Q + NA_WIDTH
COL_V = COL_K + NA_WIDTH
COL_Z = COL_V + NA_WIDTH
COL_XBC = COL_Z + SSM_INNER
COL_DT = COL_XBC + SSM_CONV_CH
COL_POOL = COL_DT + 2 * SSM_HEADS
IN_COLS = COL_POOL + POOL_WIDTH
N_EXPERTS = 8

LANES = 128
VMEM_LIMIT = 56 * 1024 * 1024

SSD_CHUNK = 128
HALO = 16
LOCAL_TILE = 256
CONV_ROWS = 64
NEG = -1e30
LAT_Q_ROWS = 8
LAT_K_ROWS = 16
MOE_GROUP = 512
ROW_TILES = D_MODEL // LANES


def _cparams(sem, vmem=VMEM_LIMIT):
    return pltpu.CompilerParams(dimension_semantics=sem, vmem_limit_bytes=vmem)


def _const_spec(shape):
    nd = len(shape)
    return pl.BlockSpec(shape, lambda *_: (0,) * nd, pipeline_mode=pl.Buffered(1))


def _mod_row(i, tm, n_ctx_tok, lat_len):
    n_ctx_tiles = n_ctx_tok // tm
    return jnp.where(i < n_ctx_tiles, 0, 1 + (i - n_ctx_tiles) // (lat_len // tm))


def _silu(x):
    return x * jax.nn.sigmoid(x)


def _store_token_major(ref, val):
    rows = val.shape[0]
    for k in range(ROW_TILES):
        ref[pl.ds(k, rows, stride=ROW_TILES), :] = val[:, k * LANES:(k + 1) * LANES]


def _load_token_major(ref, rows, k):
    return ref[pl.ds(k, rows, stride=ROW_TILES), :]


def _split2(x):
    hi = x.astype(BF16)
    lo = (x - hi.astype(F32)).astype(BF16)
    return hi, lo


def _split3(x):
    hi = x.astype(BF16)
    r = x - hi.astype(F32)
    mid = r.astype(BF16)
    lo = (r - mid.astype(F32)).astype(BF16)
    return hi, mid, lo


def _ada_kernel(c_ref, w_ref, b_ref, o_ref):
    cb = _silu(c_ref[...]).astype(BF16)
    o_ref[...] = jnp.dot(cb, w_ref[...].astype(BF16), preferred_element_type=F32) + b_ref[...]


def ada_modulation(cond, ada_w, ada_b):
    depth, d, n = ada_w.shape
    r = cond.shape[0]
    tn = 1024
    return pl.pallas_call(
        _ada_kernel,
        out_shape=jax.ShapeDtypeStruct((depth, r, n), F32),
        grid=(depth, n // tn),
        in_specs=[pl.BlockSpec((r, d), lambda l, j: (0, 0)),
                  pl.BlockSpec((None, d, tn), lambda l, j: (l, 0, j)),
                  pl.BlockSpec((None, 1, tn), lambda l, j: (l, 0, j))],
        out_specs=pl.BlockSpec((None, r, tn), lambda l, j: (l, 0, j)),
        compiler_params=_cparams(("arbitrary", "arbitrary")),
        name="ada_modulation",
    )(cond, ada_w, ada_b.reshape(depth, 1, n))


def _modulated_norm(x, nw, shift, scale):
    y = x * lax.rsqrt(jnp.mean(x * x, axis=-1, keepdims=True) + NORM_EPS)
    return (y * nw) * (1.0 + scale) + shift


def _head_rmsnorm(acc, w_row, eblk):
    sq = (acc * acc).astype(BF16)
    half = eblk.shape[0]
    parts = [jnp.dot(sq[:, j * half:(j + 1) * half], eblk, preferred_element_type=F32)
             for j in range(NA_WIDTH // half)]
    ms = jnp.concatenate(parts, axis=-1) * (1.0 / NA_HEAD_DIM)
    return (acc * lax.rsqrt(ms + NORM_EPS)) * w_row


def _softplus(x):
    return jnp.maximum(x, 0.0) + jnp.log(1.0 + jnp.exp(-jnp.abs(x)))


def _proj_kernel(x_ref, mod_ref, nw_ref, wq_ref, wk_ref, wv_ref, wz_ref, wxbc_ref, wdt_ref, wu_ref,
                 qn_ref, kn_ref, dtb_ref, eblk_ref,
                 q_out, k_out, v_out, kb_out, vb_out, z_out, xbc_out, dt_out, u_out):
    h = _modulated_norm(x_ref[...], nw_ref[...], mod_ref[0:1, :], mod_ref[1:2, :])
    hb = h.astype(BF16)
    dot = lambda w_ref: jnp.dot(hb, w_ref[...], preferred_element_type=F32)
    eblk = eblk_ref[...]
    q = _head_rmsnorm(dot(wq_ref), qn_ref[...], eblk)
    q_out[...] = (q * NA_SCALE).astype(BF16)
    k = _head_rmsnorm(dot(wk_ref), kn_ref[...], eblk)
    k_out[...] = k
    kb_out[...] = k.astype(BF16)
    v = dot(wv_ref)
    v_out[...] = v
    vb_out[...] = v.astype(BF16)
    z_out[...] = dot(wz_ref).astype(BF16)
    xbc_out[...] = dot(wxbc_ref).astype(BF16)
    dt_out[...] = _softplus(dot(wdt_ref) + dtb_ref[...])
    u_out[...] = dot(wu_ref).astype(BF16)


def input_projection(x, mod, nw, w_in, qn, kn, dt_bias, n_ctx_tok, lat_len, tm=512):
    t, d = x.shape
    wb = w_in.astype(BF16)
    wq, wk, wv = wb[:, COL_Q:COL_K], wb[:, COL_K:COL_V], wb[:, COL_V:COL_Z]
    wz, wxbc, wu = wb[:, COL_Z:COL_XBC], wb[:, COL_XBC:COL_DT], wb[:, COL_POOL:]
    wdt = jnp.pad(wb[:, COL_DT:COL_POOL], ((0, 0), (0, LANES - 2 * SSM_HEADS)))
    dtb = jnp.pad(dt_bias.reshape(1, 2 * SSM_HEADS), ((0, 0), (0, LANES - 2 * SSM_HEADS)))
    qn_row = jnp.tile(qn, NA_HEADS).reshape(1, NA_WIDTH)
    kn_row = jnp.tile(kn, NA_HEADS).reshape(1, NA_WIDTH)
    half = 256
    eblk = jnp.asarray(np.kron(np.eye(half // NA_HEAD_DIM), np.ones((NA_HEAD_DIM, NA_HEAD_DIM))), BF16)
    row = lambda i: (i, 0)
    tok = lambda w, dt_: (pl.BlockSpec((tm, w), row), jax.ShapeDtypeStruct((t, w), dt_))
    outs = [tok(NA_WIDTH, BF16), tok(NA_WIDTH, F32), tok(NA_WIDTH, F32), tok(NA_WIDTH, BF16),
            tok(NA_WIDTH, BF16), tok(SSM_INNER, BF16), tok(SSM_CONV_CH, BF16), tok(LANES, F32),
            tok(POOL_WIDTH, BF16)]
    consts = [nw.reshape(1, d), wq, wk, wv, wz, wxbc, wdt, wu, qn_row, kn_row, dtb, eblk]
    return pl.pallas_call(
        _proj_kernel,
        out_shape=[o[1] for o in outs],
        grid=(t // tm,),
        in_specs=[pl.BlockSpec((tm, d), row),
                  pl.BlockSpec((None, 6, d), lambda i: (_mod_row(i, tm, n_ctx_tok, lat_len), 0, 0))]
                 + [_const_spec(c.shape) for c in consts],
        out_specs=[o[0] for o in outs],
        compiler_params=_cparams(("parallel",)),
        name="input_projection",
    )(x, mod, *consts)


def _gate_kernel(x_ref, mod_ref, nw_ref, wg_ref, bg_ref, g_out):
    h = _modulated_norm(x_ref[...], nw_ref[...], mod_ref[0:1, :], mod_ref[1:2, :])
    hb = h.astype(BF16)
    d = x_ref.shape[1]
    for j in range(N_BRANCH):
        sl = slice(j * d, (j + 1) * d)
        acc = jnp.dot(hb, wg_ref[:, sl], preferred_element_type=F32) + bg_ref[:, sl]
        g_out[:, sl] = jax.nn.sigmoid(acc).astype(BF16)


def branch_gates(x, mod, nw, w_gate, b_gate, n_ctx_tok, lat_len, tm=512):
    t, d = x.shape
    n = w_gate.shape[1]
    consts = [nw.reshape(1, d), w_gate.astype(BF16), b_gate.reshape(1, n)]
    return pl.pallas_call(
        _gate_kernel,
        out_shape=jax.ShapeDtypeStruct((t, n), BF16),
        grid=(t // tm,),
        in_specs=[pl.BlockSpec((tm, d), lambda i: (i, 0)),
                  pl.BlockSpec((None, 6, d), lambda i: (_mod_row(i, tm, n_ctx_tok, lat_len), 0, 0))]
                 + [_const_spec(c.shape) for c in consts],
        out_specs=pl.BlockSpec((tm, n), lambda i: (i, 0)),
        compiler_params=_cparams(("parallel",)),
        name="branch_gates",
    )(x, mod, *consts)


def _nt_dot(a, b):
    return lax.dot_general(a, b, (((1,), (1,)), ((), ())), preferred_element_type=F32)


def _head_masks(dtype):
    lane = lax.broadcasted_iota(jnp.int32, (1, LANES), 1)
    return [((lane >= j * NA_HEAD_DIM) & (lane < (j + 1) * NA_HEAD_DIM)).astype(dtype)
            for j in range(LANES // NA_HEAD_DIM)]


def _ctx_attn_kernel(q_ref, k_ref, v_ref, o_ref):
    q, k, v = q_ref[...], k_ref[...], v_ref[...]
    out = jnp.zeros(o_ref.shape, F32)
    for hm in _head_masks(BF16):
        s = _nt_dot(q * hm, k)
        p = jnp.exp(s - jnp.max(s, axis=-1, keepdims=True))
        den = jnp.sum(p, axis=-1, keepdims=True)
        out = out + jnp.dot(p.astype(BF16), v * hm, preferred_element_type=F32) / den
    o_ref[...] = out.astype(BF16)


def context_attention(q, kb, vb, n_seq, seq_len):
    blk = lambda: pl.BlockSpec((seq_len, LANES), lambda b, hp: (b, hp))
    return pl.pallas_call(
        _ctx_attn_kernel,
        out_shape=jax.ShapeDtypeStruct((n_seq * seq_len, NA_WIDTH), BF16),
        grid=(n_seq, NA_WIDTH // LANES),
        in_specs=[blk(), blk(), blk()],
        out_specs=blk(),
        compiler_params=_cparams(("parallel", "parallel")),
        name="context_attention",
    )(q, kb, vb)


def _lat_attn_kernel(q_ref, k_ref, v_ref, kc_ref, vc_ref, bias_ref, o_ref):
    rb = pl.program_id(1)
    n_rows = k_ref.shape[0] // GRID_W
    ks = jnp.clip(rb * LAT_Q_ROWS - NA_KH // 2, 0, n_rows - LAT_K_ROWS) * GRID_W
    ks = pl.multiple_of(ks, GRID_W)
    kw = k_ref[pl.ds(ks, LAT_K_ROWS * GRID_W), :]
    vw = v_ref[pl.ds(ks, LAT_K_ROWS * GRID_W), :]
    kc = kc_ref[...].astype(BF16)
    vc = vc_ref[...].astype(BF16)
    q = q_ref[...]
    out = jnp.zeros(o_ref.shape, F32)
    for j, hm in enumerate(_head_masks(BF16)):
        qh = q * hm
        s1 = _nt_dot(qh, kw) + bias_ref[j].astype(F32)
        s2 = _nt_dot(qh, kc)
        m = jnp.maximum(jnp.max(s1, axis=-1, keepdims=True), jnp.max(s2, axis=-1, keepdims=True))
        p1 = jnp.exp(s1 - m)
        p2 = jnp.exp(s2 - m)
        den = jnp.sum(p1, axis=-1, keepdims=True) + jnp.sum(p2, axis=-1, keepdims=True)
        o = (jnp.dot(p1.astype(BF16), vw * hm, preferred_element_type=F32)
             + jnp.dot(p2.astype(BF16), vc * hm, preferred_element_type=F32))
        out = out + o / den
    o_ref[...] = out.astype(BF16)


def _latent_bias_table(rpb, rows):
    col = np.arange(GRID_W)
    cs = np.clip(col - NA_KW // 2, 0, GRID_W - NA_KW)
    kc = np.arange(GRID_W)
    valid_col = (kc[None, :] >= cs[:, None]) & (kc[None, :] < cs[:, None] + NA_KW)
    dcol = np.clip(kc[None, :] - col[:, None] + NA_KW - 1, 0, 2 * NA_KW - 2)
    toe = jnp.where(valid_col[None, None], rpb[:, :, dcol], NEG)
    toe = jnp.concatenate([toe, jnp.full_like(toe[:, :1], NEG)], axis=1)
    n_d = 2 * NA_KH - 1
    tile_idx = np.full((3, LAT_Q_ROWS, LAT_K_ROWS), n_d, np.int32)
    for ty, r0 in enumerate((0, LAT_Q_ROWS, rows - LAT_Q_ROWS)):
        ks = int(np.clip(r0 - NA_KH // 2, 0, rows - LAT_K_ROWS))
        for qr in range(LAT_Q_ROWS):
            r = r0 + qr
            rs = int(np.clip(r - NA_KH // 2, 0, rows - NA_KH))
            for kr in range(LAT_K_ROWS):
                if rs <= ks + kr < rs + NA_KH:
                    tile_idx[ty, qr, kr] = ks + kr - r + NA_KH - 1
    t = jnp.take(toe, jnp.asarray(tile_idx.reshape(-1)), axis=1)
    t = t.reshape(NA_HEADS, 3, LAT_Q_ROWS, LAT_K_ROWS, GRID_W, GRID_W)
    t = jnp.transpose(t, (1, 0, 2, 4, 3, 5))
    return t.reshape(3, NA_HEADS, LAT_Q_ROWS * GRID_W, LAT_K_ROWS * GRID_W).astype(BF16)


def latent_attention(q, kb, vb, k_ctx, v_ctx, rpb, n_ctx_tok, n_seq, seq_len):
    rows = seq_len // GRID_W
    n_rb = rows // LAT_Q_ROWS
    tq = LAT_Q_ROWS * GRID_W
    lc = k_ctx.shape[1]
    bias = _latent_bias_table(rpb, rows)
    hp_n = NA_WIDTH // LANES
    heads_per = LANES // NA_HEAD_DIM
    q0 = n_ctx_tok // tq
    s0 = n_ctx_tok // seq_len
    btype = lambda rb: jnp.where(rb == 0, 0, jnp.where(rb == n_rb - 1, 2, 1))
    return pl.pallas_call(
        _lat_attn_kernel,
        out_shape=jax.ShapeDtypeStruct((n_seq * seq_len, NA_WIDTH), BF16),
        grid=(hp_n, n_rb, n_seq),
        in_specs=[pl.BlockSpec((tq, LANES), lambda hp, rb, b: (q0 + b * n_rb + rb, hp)),
                  pl.BlockSpec((seq_len, LANES), lambda hp, rb, b: (s0 + b, hp)),
                  pl.BlockSpec((seq_len, LANES), lambda hp, rb, b: (s0 + b, hp)),
                  pl.BlockSpec((None, lc, LANES), lambda hp, rb, b: (b, 0, hp)),
                  pl.BlockSpec((None, lc, LANES), lambda hp, rb, b: (b, 0, hp)),
                  pl.BlockSpec((None, heads_per, tq, LAT_K_ROWS * GRID_W),
                               lambda hp, rb, b: (btype(rb), hp, 0, 0))],
        out_specs=pl.BlockSpec((tq, LANES), lambda hp, rb, b: (b * n_rb + rb, hp)),
        compiler_params=_cparams(("arbitrary", "arbitrary", "arbitrary")),
        name="latent_attention",
    )(q, kb, vb, k_ctx, v_ctx, bias)


def _local_kernel(seq_tiles_ref, xc_ref, xp_ref, xn_ref, uc_ref, up_ref, un_ref, cw_ref, cb_ref,
                  bandc_ref, bandp_ref, bandn_ref, pw_ref, ps_ref, xbc_out, p_out, ext_ref):
    i = pl.program_id(0)
    tl = xc_ref.shape[0]
    j = seq_tiles_ref[0, i]
    nt = seq_tiles_ref[1, i]
    has_prev = (j > 0).astype(F32)
    has_next = (j < nt - 1).astype(F32)

    ext_ref[0:HALO, :] = xp_ref[...].astype(F32) * has_prev
    ext_ref[HALO:HALO + tl, :] = xc_ref[...].astype(F32)
    ext_ref[HALO + tl:, :] = xn_ref[...].astype(F32) * has_next
    pad = SSM_CONV // 2
    for lt in range(xc_ref.shape[1] // LANES):
        ls = slice(lt * LANES, (lt + 1) * LANES)
        taps = [cw_ref[k:k + 1, ls] for k in range(SSM_CONV)]
        bias = cb_ref[:, ls]
        for rb in range(tl // CONV_ROWS):
            r0 = HALO + rb * CONV_ROWS - pad
            acc = bias + ext_ref[r0:r0 + CONV_ROWS, ls] * taps[0]
            for k in range(1, SSM_CONV):
                acc = acc + ext_ref[r0 + k:r0 + k + CONV_ROWS, ls] * taps[k]
            xbc_out[rb * CONV_ROWS:(rb + 1) * CONV_ROWS, ls] = _silu(acc).astype(BF16)

    t = j * tl + lax.broadcasted_iota(jnp.int32, (tl, 1), 0)
    seq_len = nt * tl
    uprev = (up_ref[...].astype(F32) * has_prev).astype(BF16)
    unext = (un_ref[...].astype(F32) * has_next).astype(BF16)
    for g, w in enumerate(POOL_WINDOWS):
        sl = slice(g * POOL_GROUP, (g + 1) * POOL_GROUP)
        ucur = uc_ref[:, sl]
        ssum = (jnp.dot(bandc_ref[g], ucur, preferred_element_type=F32)
                + jnp.dot(bandp_ref[g], uprev[:, sl], preferred_element_type=F32)
                + jnp.dot(bandn_ref[g], unext[:, sl], preferred_element_type=F32))
        lo = jnp.clip(t - w // 2, 0, seq_len)
        hi = jnp.clip(t - w // 2 + w, 0, seq_len)
        pooled = ssum / (hi - lo).astype(F32) - ucur.astype(F32)
        mixed = jnp.dot(pooled.astype(BF16), pw_ref[g], preferred_element_type=F32)
        p_out[:, sl] = (mixed * ps_ref[:, sl]).astype(BF16)


def _pool_bands(tl):
    t = np.arange(tl)[:, None]
    bc = np.zeros((len(POOL_WINDOWS), tl, tl), np.float32)
    bp = np.zeros((len(POOL_WINDOWS), tl, HALO), np.float32)
    bn = np.zeros((len(POOL_WINDOWS), tl, HALO), np.float32)
    for g, w in enumerate(POOL_WINDOWS):
        lo, hi = t - w // 2, t - w // 2 + w
        jc = np.arange(tl)[None, :]
        bc[g] = (jc >= lo) & (jc < hi)
        jp = np.arange(-HALO, 0)[None, :]
        bp[g] = (jp >= lo) & (jp < hi)
        jn = np.arange(tl, tl + HALO)[None, :]
        bn[g] = (jn >= lo) & (jn < hi)
    return jnp.asarray(bc, BF16), jnp.asarray(bp, BF16), jnp.asarray(bn, BF16)


def _seq_tile_table(seq_lens, tile):
    pos, cnt = [], []
    for sl in seq_lens:
        n = sl // tile
        pos += list(range(n))
        cnt += [n] * n
    return np.asarray([pos, cnt], np.int32)


def local_mixers(xbc, u, conv_w, conv_b, pool_w, pool_scale, seq_lens):
    t, c = xbc.shape
    tl = LOCAL_TILE
    table = jnp.asarray(_seq_tile_table(seq_lens, tl))
    hb = tl // HALO
    n_halo = t // HALO
    cur = lambda w: pl.BlockSpec((tl, w), lambda i, s: (i, 0))
    prev = lambda w: pl.BlockSpec((HALO, w), lambda i, s: (jnp.maximum(i * hb - 1, 0), 0))
    nxt = lambda w: pl.BlockSpec((HALO, w), lambda i, s: (jnp.minimum((i + 1) * hb, n_halo - 1), 0))
    cst = lambda a: pl.BlockSpec(a.shape, lambda i, s: (0,) * a.ndim, pipeline_mode=pl.Buffered(1))
    bc, bp, bn = _pool_bands(tl)
    consts = [conv_w, conv_b.reshape(1, c), bc, bp, bn, pool_w.astype(BF16), pool_scale.reshape(1, POOL_WIDTH)]
    return pl.pallas_call(
        _local_kernel,
        out_shape=[jax.ShapeDtypeStruct((t, c), BF16), jax.ShapeDtypeStruct((t, POOL_WIDTH), BF16)],
        grid_spec=pltpu.PrefetchScalarGridSpec(
            num_scalar_prefetch=1,
            grid=(t // tl,),
            in_specs=[cur(c), prev(c), nxt(c), cur(POOL_WIDTH), prev(POOL_WIDTH), nxt(POOL_WIDTH)]
                     + [cst(a) for a in consts],
            out_specs=[cur(c), cur(POOL_WIDTH)],
            scratch_shapes=[pltpu.VMEM((tl + 2 * HALO, c), F32)]),
        compiler_params=_cparams(("arbitrary",)),
        name="local_mixers",
    )(table, xbc, xbc, xbc, u, u, u, *consts)


def _ssd_direction(xbc_ref, dt_ref, alog_ref, expand_ref, state_ref, y_out, reverse):
    cl = xbc_ref.shape[0]
    gw = SSM_INNER // SSM_GROUPS
    lane0 = SSM_HEADS if reverse else 0
    xs = xbc_ref[:, :SSM_INNER].astype(F32)
    bm = xbc_ref[:, SSM_INNER:SSM_INNER + SSM_GROUPS * SSM_STATE]
    cm = xbc_ref[:, SSM_INNER + SSM_GROUPS * SSM_STATE:]
    dt = dt_ref[...]
    a = dt * (-jnp.exp(alog_ref[...]))
    ri = lax.broadcasted_iota(jnp.int32, (cl, cl), 0)
    ci = lax.broadcasted_iota(jnp.int32, (cl, cl), 1)
    tri = (ci >= ri) if reverse else (ci <= ri)
    trib = tri.astype(BF16)
    acum = sum(jnp.dot(trib, part, preferred_element_type=F32) for part in _split3(a))
    acum_t = acum.T
    last = 0 if reverse else cl - 1
    atot = acum[last:last + 1, :]
    stacked = jnp.concatenate([dt, jnp.exp(acum), jnp.exp(atot - acum)], axis=0)
    ex = sum(jnp.dot(part, expand_ref[...], preferred_element_type=F32) for part in _split2(stacked))
    dt_e, eacum_e, wend_e = ex[:cl], ex[cl:2 * cl], ex[2 * cl:]
    xdt = xs * dt_e
    xdt_b = xdt.astype(BF16)
    lane = lax.broadcasted_iota(jnp.int32, (1, LANES), 1)
    left = lane < SSM_HEAD_DIM
    state = state_ref[...]
    state_b = state.astype(BF16)
    y_parts = []
    for g in range(SSM_GROUPS):
        bg = bm[:, g * SSM_STATE:(g + 1) * SSM_STATE]
        cg = cm[:, g * SSM_STATE:(g + 1) * SSM_STATE]
        cb = _nt_dot(cg, bg)
        y_off = jnp.dot(cg, state_b[:, g * gw:(g + 1) * gw], preferred_element_type=F32)
        heads_g = SSM_HEADS // SSM_GROUPS
        diag = []
        for pair in range(heads_g // 2):
            ms = []
            for hh in range(2):
                hl = lane0 + g * heads_g + pair * 2 + hh
                seg = acum[:, hl:hl + 1] - acum_t[hl:hl + 1, :]
                ms.append((cb * jnp.where(tri, jnp.exp(jnp.minimum(seg, 0.0)), 0.0)).astype(BF16))
            c0 = g * gw + pair * LANES
            xp = xdt_b[:, c0:c0 + LANES]
            rhs = jnp.concatenate([jnp.where(left, xp, jnp.zeros_like(xp)),
                                   jnp.where(left, jnp.zeros_like(xp), xp)], axis=0)
            diag.append(jnp.dot(jnp.concatenate(ms, axis=1), rhs, preferred_element_type=F32))
        y_parts.append(jnp.concatenate(diag, axis=1) + y_off * eacum_e[:, g * gw:(g + 1) * gw])
        xw = (xdt[:, g * gw:(g + 1) * gw] * wend_e[:, g * gw:(g + 1) * gw]).astype(BF16)
        upd = jnp.dot(bg.astype(F32).T.astype(BF16), xw, preferred_element_type=F32)
        state_ref[:, g * gw:(g + 1) * gw] = (state[:, g * gw:(g + 1) * gw]
                                             * eacum_e[last:last + 1, g * gw:(g + 1) * gw] + upd)
    y_out[...] = jnp.concatenate(y_parts, axis=1).astype(y_out.dtype)


def _ssd_kernel(tab_ref, xf_ref, xb_ref, dtf_ref, dtb_ref, alog_ref, ef_ref, eb_ref, h0f_ref, h0b_ref,
                yf_out, yb_out, sf_out, sb_out, state_f, state_b):
    g = pl.program_id(0)

    @pl.when(tab_ref[1, g] == 1)
    def _():
        state_f[...] = h0f_ref[...].T
        state_b[...] = h0b_ref[...].T

    _ssd_direction(xf_ref, dtf_ref, alog_ref, ef_ref, state_f, yf_out, reverse=False)
    _ssd_direction(xb_ref, dtb_ref, alog_ref, eb_ref, state_b, yb_out, reverse=True)

    @pl.when(tab_ref[2, g] == 1)
    def _():
        sf_out[...] = state_f[...].T
        sb_out[...] = state_b[...].T


def _ssd_tables(seq_lens, cl):
    bwd, first, last, seq = [], [], [], []
    start = 0
    for s, sl in enumerate(seq_lens):
        nc = sl // cl
        for c in range(nc):
            bwd.append(start + nc - 1 - c)
            first.append(int(c == 0))
            last.append(int(c == nc - 1))
            seq.append(s)
        start += nc
    return np.asarray([bwd, first, last, seq], np.int32)


def ssd_scan(xbc, dt, a_log, h0_f, h0_b, seq_lens):
    t = xbc.shape[0]
    cl = SSD_CHUNK
    n_seq = len(seq_lens)
    tab = jnp.asarray(_ssd_tables(seq_lens, cl))
    alog = jnp.pad(a_log.reshape(1, 2 * SSM_HEADS), ((0, 0), (0, LANES - 2 * SSM_HEADS)))
    ef = np.zeros((LANES, SSM_INNER), np.float32)
    eb = np.zeros((LANES, SSM_INNER), np.float32)
    for h in range(SSM_HEADS):
        ef[h, h * SSM_HEAD_DIM:(h + 1) * SSM_HEAD_DIM] = 1.0
        eb[SSM_HEADS + h, h * SSM_HEAD_DIM:(h + 1) * SSM_HEAD_DIM] = 1.0
    fwd = lambda w: pl.BlockSpec((cl, w), lambda g, tb: (g, 0))
    bwd = lambda w: pl.BlockSpec((cl, w), lambda g, tb: (tb[0, g], 0))
    cst = lambda shape: pl.BlockSpec(shape, lambda g, tb: (0,) * len(shape), pipeline_mode=pl.Buffered(1))
    st = lambda: pl.BlockSpec((None, SSM_INNER, SSM_STATE), lambda g, tb: (tb[3, g], 0, 0))
    return pl.pallas_call(
        _ssd_kernel,
        out_shape=[jax.ShapeDtypeStruct((t, SSM_INNER), BF16), jax.ShapeDtypeStruct((t, SSM_INNER), BF16),
                   jax.ShapeDtypeStruct((n_seq, SSM_INNER, SSM_STATE), F32),
                   jax.ShapeDtypeStruct((n_seq, SSM_INNER, SSM_STATE), F32)],
        grid_spec=pltpu.PrefetchScalarGridSpec(
            num_scalar_prefetch=1,
            grid=(t // cl,),
            in_specs=[fwd(SSM_CONV_CH), bwd(SSM_CONV_CH), fwd(LANES), bwd(LANES),
                      cst((1, LANES)), cst((LANES, SSM_INNER)), cst((LANES, SSM_INNER)), st(), st()],
            out_specs=[fwd(SSM_INNER), bwd(SSM_INNER), st(), st()],
            scratch_shapes=[pltpu.VMEM((SSM_STATE, SSM_INNER), F32), pltpu.VMEM((SSM_STATE, SSM_INNER), F32)]),
        compiler_params=_cparams(("arbitrary",)),
        name="ssd_scan",
    )(tab, xbc, xbc, dt, dt, alog, jnp.asarray(ef, BF16), jnp.asarray(eb, BF16), h0_f, h0_b)


def _merge_kernel(x_ref, mod_ref, a_ref, yf_ref, yb_ref, xs_ref, z_ref, p_ref, g_ref,
                  dsum_ref, snw_ref, wna_ref, wssm_ref, wpool_ref, wout_ref, n2_ref, *rest,
                  with_router):
    if with_router:
        router_ref, x1_out, h2_out, logit_out = rest
    else:
        x1_out, h2_out = rest
    d = x_ref.shape[1]
    y = yf_ref[...].astype(F32) + yb_ref[...].astype(F32) + dsum_ref[...] * xs_ref[...].astype(F32)
    y = y * _silu(z_ref[...].astype(F32))
    gw = SSM_INNER // SSM_GROUPS
    parts = []
    for g in range(SSM_GROUPS):
        yg = y[:, g * gw:(g + 1) * gw]
        yg = yg * lax.rsqrt(jnp.mean(yg * yg, axis=-1, keepdims=True) + NORM_EPS)
        parts.append(yg * snw_ref[:, g * gw:(g + 1) * gw])
    s = jnp.concatenate(parts, axis=1).astype(BF16)
    gate = lambda j: g_ref[:, j * d:(j + 1) * d].astype(F32)
    merged = (gate(0) * jnp.dot(a_ref[...], wna_ref[...], preferred_element_type=F32)
              + gate(1) * jnp.dot(s, wssm_ref[...], preferred_element_type=F32)
              + gate(2) * jnp.dot(p_ref[...], wpool_ref[...], preferred_element_type=F32))
    x1 = x_ref[...] + mod_ref[2:3, :] * jnp.dot(merged.astype(BF16), wout_ref[...],
                                                preferred_element_type=F32)
    x1_out[...] = x1
    h2 = _modulated_norm(x1, n2_ref[...], mod_ref[3:4, :], mod_ref[4:5, :])
    if with_router:
        _store_token_major(h2_out, h2)
        hh, hl = _split2(h2)
        rh, rl = router_ref[0], router_ref[1]
        logit_out[...] = (jnp.dot(hh, rh, preferred_element_type=F32)
                          + jnp.dot(hl, rh, preferred_element_type=F32)
                          + jnp.dot(hh, rl, preferred_element_type=F32))
    else:
        h2_out[...] = h2.astype(BF16)


def merge_branches(x, mod, a, yf, yb, xbc, z, p, gates, d_sum, ssm_norm_w, wb_na, wb_ssm, wb_pool, w_out,
                   norm2_w, router, n_ctx_tok, lat_len, tm=512):
    t, d = x.shape
    row = lambda i: (i, 0)
    consts = [jnp.repeat(d_sum, SSM_HEAD_DIM).reshape(1, SSM_INNER), ssm_norm_w.reshape(1, SSM_INNER),
              wb_na.astype(BF16), wb_ssm.astype(BF16), wb_pool.astype(BF16), w_out.astype(BF16),
              norm2_w.reshape(1, d)]
    out_shape = [jax.ShapeDtypeStruct((t, d), F32)]
    out_specs = [pl.BlockSpec((tm, d), row)]
    with_router = router is not None
    if with_router:
        rpad = jnp.pad(router, ((0, 0), (0, LANES - router.shape[1])))
        rh = rpad.astype(BF16)
        rl = (rpad - rh.astype(F32)).astype(BF16)
        consts.append(jnp.stack([rh, rl]))
        out_shape += [jax.ShapeDtypeStruct((t * ROW_TILES, LANES), F32), jax.ShapeDtypeStruct((t, LANES), F32)]
        out_specs += [pl.BlockSpec((tm * ROW_TILES, LANES), row), pl.BlockSpec((tm, LANES), row)]
    else:
        out_shape += [jax.ShapeDtypeStruct((t, d), BF16)]
        out_specs += [pl.BlockSpec((tm, d), row)]
    return pl.pallas_call(
        functools.partial(_merge_kernel, with_router=with_router),
        out_shape=out_shape,
        grid=(t // tm,),
        in_specs=[pl.BlockSpec((tm, d), row),
                  pl.BlockSpec((None, 6, d), lambda i: (_mod_row(i, tm, n_ctx_tok, lat_len), 0, 0)),
                  pl.BlockSpec((tm, NA_WIDTH), row), pl.BlockSpec((tm, SSM_INNER), row),
                  pl.BlockSpec((tm, SSM_INNER), row), pl.BlockSpec((tm, SSM_INNER), row),
                  pl.BlockSpec((tm, SSM_INNER), row), pl.BlockSpec((tm, POOL_WIDTH), row),
                  pl.BlockSpec((tm, N_BRANCH * d), row)]
                 + [_const_spec(c.shape) for c in consts],
        out_specs=out_specs,
        compiler_params=_cparams(("parallel",)),
        name="merge_branches",
    )(x, mod, a, yf, yb, xbc, z, p, gates, *consts)


def _ffn_kernel(h_ref, x_ref, mod_ref, w1_ref, w3_ref, w2_ref, o_ref, acc_ref, *, tf):
    f = w1_ref.shape[1]
    hb = h_ref[...]
    for c in range(f // tf):
        sl = slice(c * tf, (c + 1) * tf)
        u = (_silu(jnp.dot(hb, w1_ref[:, sl], preferred_element_type=F32))
             * jnp.dot(hb, w3_ref[:, sl], preferred_element_type=F32))
        part = jnp.dot(u.astype(BF16), w2_ref[sl, :], preferred_element_type=F32)
        if c == 0:
            acc_ref[...] = part
        else:
            acc_ref[...] += part
    o_ref[...] = x_ref[...] + mod_ref[5:6, :] * acc_ref[...]


def dense_ffn(h2, x1, mod, w1, w3, w2, n_ctx_tok, lat_len, tm=512, tf=256):
    t, d = x1.shape
    consts = [w1.astype(BF16), w3.astype(BF16), w2.astype(BF16)]
    return pl.pallas_call(
        functools.partial(_ffn_kernel, tf=tf),
        out_shape=jax.ShapeDtypeStruct((t, d), F32),
        grid=(t // tm,),
        in_specs=[pl.BlockSpec((tm, d), lambda i: (i, 0)), pl.BlockSpec((tm, d), lambda i: (i, 0)),
                  pl.BlockSpec((None, 6, d), lambda i: (_mod_row(i, tm, n_ctx_tok, lat_len), 0, 0))]
                 + [_const_spec(c.shape) for c in consts],
        out_specs=pl.BlockSpec((tm, d), lambda i: (i, 0)),
        scratch_shapes=[pltpu.VMEM((tm, d), F32)],
        compiler_params=_cparams(("parallel",)),
        name="dense_ffn",
    )(h2, x1, mod, *consts)


def _route_kernel(logit_ref, info_out, gate_out, count_out, base_ref):
    i = pl.program_id(0)

    @pl.when(i == 0)
    def _():
        base_ref[...] = jnp.zeros_like(base_ref)

    lg = logit_ref[...]
    tm = lg.shape[0]
    lane = lax.broadcasted_iota(jnp.int32, lg.shape, 1)
    valid = lane < N_EXPERTS
    lg = jnp.where(valid, lg, -jnp.inf)
    m1 = jnp.max(lg, axis=-1, keepdims=True)
    e1 = jnp.min(jnp.where(lg == m1, lane, LANES), axis=-1, keepdims=True)
    lg2 = jnp.where(lane == e1, -jnp.inf, lg)
    m2 = jnp.max(lg2, axis=-1, keepdims=True)
    e2 = jnp.min(jnp.where(lg2 == m2, lane, LANES), axis=-1, keepdims=True)
    g1 = 1.0 / (1.0 + jnp.exp(m2 - m1))
    oh1 = (lane == e1).astype(F32)
    oh2 = (lane == e2).astype(F32)
    oh = oh1 + oh2
    ri = lax.broadcasted_iota(jnp.int32, (tm, tm), 0)
    ci = lax.broadcasted_iota(jnp.int32, (tm, tm), 1)
    before = jnp.dot((ci < ri).astype(BF16), oh.astype(BF16), preferred_element_type=F32) + base_ref[...]
    r1 = jnp.sum(oh1 * before, axis=-1, keepdims=True)
    r2 = jnp.sum(oh2 * before, axis=-1, keepdims=True)
    lane8 = lax.broadcasted_iota(jnp.int32, (tm, 8), 1)
    info = jnp.where(lane8 == 0, e1, jnp.where(lane8 == 1, e2,
                     jnp.where(lane8 == 2, r1.astype(jnp.int32), r2.astype(jnp.int32))))
    info_out[...] = info
    gate_out[...] = jnp.where(lane8 == 0, g1, 1.0 - g1)
    base_ref[...] += jnp.sum(oh, axis=0, keepdims=True)
    count_out[...] = base_ref[...].astype(jnp.int32)


def route_tokens(logits, tm=512):
    t = logits.shape[0]
    return pl.pallas_call(
        _route_kernel,
        out_shape=[jax.ShapeDtypeStruct((t, 8), jnp.int32), jax.ShapeDtypeStruct((t, 8), F32),
                   jax.ShapeDtypeStruct((1, LANES), jnp.int32)],
        grid=(t // tm,),
        in_specs=[pl.BlockSpec((tm, LANES), lambda i: (i, 0))],
        out_specs=[pl.BlockSpec((tm, 8), lambda i: (i, 0)), pl.BlockSpec((tm, 8), lambda i: (i, 0)),
                   pl.BlockSpec((1, LANES), lambda i: (0, 0))],
        scratch_shapes=[pltpu.VMEM((1, LANES), F32)],
        compiler_params=_cparams(("arbitrary",)),
        name="route_tokens",
    )(logits)


def _token_rows(idx):
    return pl.ds(pl.multiple_of(idx * ROW_TILES, ROW_TILES), ROW_TILES)


def _dma_loop(copy, n, wait):
    def body(i, c):
        if wait:
            copy(i).wait()
        else:
            copy(i).start()
        return c

    lax.fori_loop(0, n, body, 0, unroll=8)


def _expert_kernel(ge_ref, tcur_ref, tnext_ref, dprev_ref, h_ref, w1_ref, w3_ref, w2_ref, y_ref,
                   ibuf, obuf, xs_ref, acc_ref, gsem, ssem):
    g = pl.program_id(0)
    f = pl.program_id(1)
    n_groups = pl.num_programs(0) - 1
    tg = xs_ref.shape[0]
    live = g < n_groups

    def gather(tok_ref, slot, n):
        return pltpu.make_async_copy(h_ref.at[_token_rows(tok_ref[0, n])], ibuf.at[slot, _token_rows(n)],
                                     gsem.at[slot])

    def scatter(n):
        return pltpu.make_async_copy(obuf.at[_token_rows(n)], y_ref.at[_token_rows(dprev_ref[0, n])], ssem)

    def partial_out():
        xb = xs_ref[...]
        u = (_silu(jnp.dot(xb, w1_ref[...], preferred_element_type=F32))
             * jnp.dot(xb, w3_ref[...], preferred_element_type=F32))
        return jnp.dot(u.astype(BF16), w2_ref[...], preferred_element_type=F32)

    @pl.when((g == 0) & (f == 0))
    def _():
        obuf[...] = jnp.zeros_like(obuf)
        _dma_loop(lambda n: gather(tcur_ref, 0, n), tg, wait=False)

    @pl.when(f == 0)
    def _():
        _dma_loop(lambda n: gather(tcur_ref, g % 2, n), tg, wait=True)

    @pl.when(live & (f == 0))
    def _():
        for k in range(ROW_TILES):
            xs_ref[:, k * LANES:(k + 1) * LANES] = _load_token_major(ibuf.at[g % 2], tg, k).astype(BF16)
        for n in range(tg):
            scatter(n).start()
        acc_ref[...] = partial_out()

    @pl.when(jnp.logical_not(live) & (f == 0))
    def _():
        _dma_loop(scatter, tg, wait=False)

    @pl.when(f == 1)
    def _():
        _dma_loop(scatter, tg, wait=True)

    @pl.when(live & (f == 1))
    def _():
        for n in range(tg):
            gather(tnext_ref, (g + 1) % 2, n).start()
        _store_token_major(obuf, acc_ref[...] + partial_out())


def expert_ffn(h2, slot_tok, slot_dst, group_expert, n_rows_out, w1, w3, w2):
    n_slots = slot_tok.shape[0]
    d, ff = w1.shape[1], w1.shape[2]
    tf = ff // 2
    tg = MOE_GROUP
    ng = n_slots // tg
    tok_tiles = slot_tok.reshape(ng, 1, tg)
    spare = n_rows_out - tg + jnp.arange(tg, dtype=jnp.int32)
    dst_tiles = jnp.concatenate([spare, slot_dst]).reshape(ng + 1, 1, tg)
    smem = lambda imap: pl.BlockSpec((None, 1, tg), imap, memory_space=pltpu.SMEM)
    gi = lambda g: jnp.minimum(g, ng - 1)
    return pl.pallas_call(
        _expert_kernel,
        out_shape=jax.ShapeDtypeStruct((n_rows_out * ROW_TILES, LANES), F32),
        grid_spec=pltpu.PrefetchScalarGridSpec(
            num_scalar_prefetch=1,
            grid=(ng + 1, 2),
            in_specs=[smem(lambda g, f, ge: (gi(g), 0, 0)),
                      smem(lambda g, f, ge: (gi(g + 1), 0, 0)),
                      smem(lambda g, f, ge: (g, 0, 0)),
                      pl.BlockSpec(memory_space=pl.ANY),
                      pl.BlockSpec((None, d, tf), lambda g, f, ge: (ge[gi(g)], 0, f)),
                      pl.BlockSpec((None, d, tf), lambda g, f, ge: (ge[gi(g)], 0, f)),
                      pl.BlockSpec((None, tf, d), lambda g, f, ge: (ge[gi(g)], f, 0))],
            out_specs=pl.BlockSpec(memory_space=pl.ANY),
            scratch_shapes=[pltpu.VMEM((2, tg * ROW_TILES, LANES), F32), pltpu.VMEM((tg * ROW_TILES, LANES), F32),
                            pltpu.VMEM((tg, d), BF16), pltpu.VMEM((tg, d), F32),
                            pltpu.SemaphoreType.DMA((2,)), pltpu.SemaphoreType.DMA(())]),
        compiler_params=_cparams(("arbitrary", "arbitrary")),
        name="expert_ffn",
    )(group_expert, tok_tiles, tok_tiles, dst_tiles, h2, w1.astype(BF16), w3.astype(BF16), w2.astype(BF16))


def _combine_kernel(y_ref, x_ref, mod_ref, gate_ref, o_ref):
    tm = x_ref.shape[0]
    g0, g1 = gate_ref[:, 0:1], gate_ref[:, 1:2]
    for k in range(ROW_TILES):
        sl = slice(k * LANES, (k + 1) * LANES)
        y = (y_ref[pl.ds(k, tm, stride=2 * ROW_TILES), :] * g0
             + y_ref[pl.ds(ROW_TILES + k, tm, stride=2 * ROW_TILES), :] * g1)
        o_ref[:, sl] = x_ref[:, sl] + mod_ref[5:6, sl] * y


def combine_rows(y2, gates, x1, mod, n_ctx_tok, lat_len, tm=256):
    t, d = x1.shape
    return pl.pallas_call(
        _combine_kernel,
        out_shape=jax.ShapeDtypeStruct((t, d), F32),
        grid=(t // tm,),
        in_specs=[pl.BlockSpec((2 * tm * ROW_TILES, LANES), lambda i: (i, 0)),
                  pl.BlockSpec((tm, d), lambda i: (i, 0)),
                  pl.BlockSpec((None, 6, d), lambda i: (_mod_row(i, tm, n_ctx_tok, lat_len), 0, 0)),
                  pl.BlockSpec((tm, 8), lambda i: (i, 0))],
        out_specs=pl.BlockSpec((tm, d), lambda i: (i, 0)),
        compiler_params=_cparams(("parallel",)),
        name="moe_combine",
    )(y2, x1, mod, gates)


def moe_ffn(h2, logits, x1, mod, w1, w3, w2, n_ctx_tok, lat_len):
    t = x1.shape[0]
    info, gates, counts = route_tokens(logits)
    counts = counts[0, :N_EXPERTS]
    padded = (counts + MOE_GROUP - 1) // MOE_GROUP * MOE_GROUP
    pad_end = jnp.cumsum(padded)
    pad_start = pad_end - padded
    experts = info[:, 0:2]
    dest = jnp.sum(jnp.where(experts[..., None] == jnp.arange(N_EXPERTS), pad_start, 0), axis=-1) + info[:, 2:4]
    n_groups = 2 * t // MOE_GROUP + N_EXPERTS
    n_slots = n_groups * MOE_GROUP
    group_start = jnp.arange(n_groups, dtype=jnp.int32) * MOE_GROUP
    group_expert = jnp.minimum(jnp.sum(group_start[:, None] >= pad_end[None, :], axis=-1), N_EXPERTS - 1)
    slot_asg = jnp.full((n_slots,), -1, jnp.int32).at[dest.reshape(-1)].set(jnp.arange(2 * t, dtype=jnp.int32))
    is_pad = (slot_asg < 0).astype(jnp.int32)
    slot_dst = jnp.where(slot_asg >= 0, slot_asg, 2 * t - 1 + jnp.cumsum(is_pad))
    slot_tok = jnp.maximum(slot_asg, 0) // 2
    y2 = expert_ffn(h2, slot_tok, slot_dst, group_expert.astype(jnp.int32), n_slots + MOE_GROUP, w1, w3, w2)
    return combine_rows(y2, gates, x1, mod, n_ctx_tok, lat_len)


def kernel(x_prompt, x_sample, c, cache_na_k, cache_na_v, state_ssd_fwd, state_ssd_bwd, c_ctx, norm1_w, norm2_w, ada_w, ada_b, w_in, na_q_norm, na_k_norm, na_rpb, ssm_conv_w, ssm_conv_b, ssm_dt_bias, ssm_a_log, ssm_d, ssm_norm_w, pool_w, pool_scale, w_branch_na, w_branch_ssm, w_branch_pool, w_gate, b_gate, w_out, ffn_w1, ffn_w3, ffn_w2, moe_router, moe_w1, moe_w3, moe_w2):
    n_ctx, ctx_len, d = x_prompt.shape
    n_lat, lat_len, _ = x_sample.shape
    depth = ada_w.shape[0]
    n_ctx_tok = n_ctx * ctx_len
    seq_lens = (ctx_len,) * n_ctx + (lat_len,) * n_lat
    lc = cache_na_k.shape[2]

    x = jnp.concatenate([x_prompt.reshape(n_ctx_tok, d), x_sample.reshape(n_lat * lat_len, d)], axis=0)
    cond = jnp.concatenate([c_ctx[None, :], c], axis=0)
    cond = jnp.pad(cond, ((0, -cond.shape[0] % 8), (0, 0)))
    mods = ada_modulation(cond, ada_w, ada_b).reshape(depth, cond.shape[0], 6, d)
    zero_state = jnp.zeros((n_ctx, SSM_INNER, SSM_STATE), F32)

    ks, vs, sfs, sbs = [], [], [], []
    for i in range(depth):
        mod = mods[i]
        q, k, v, kb, vb, z, xbc, dt, u = input_projection(
            x, mod, norm1_w[i], w_in[i], na_q_norm[i], na_k_norm[i], ssm_dt_bias[i], n_ctx_tok, lat_len)
        gates = branch_gates(x, mod, norm1_w[i], w_gate[i], b_gate[i], n_ctx_tok, lat_len)
        a_ctx = context_attention(q, kb, vb, n_ctx, ctx_len)
        a_lat = latent_attention(q, kb, vb, cache_na_k[:, i].reshape(n_lat, lc, NA_WIDTH),
                                 cache_na_v[:, i].reshape(n_lat, lc, NA_WIDTH), na_rpb[i],
                                 n_ctx_tok, n_lat, lat_len)
        a = jnp.concatenate([a_ctx, a_lat], axis=0)
        xbc_c, p = local_mixers(xbc, u, ssm_conv_w[i], ssm_conv_b[i], pool_w[i], pool_scale[i], seq_lens)
        h0_f = jnp.concatenate([zero_state, state_ssd_fwd[:, i].reshape(n_lat, SSM_INNER, SSM_STATE)], axis=0)
        h0_b = jnp.concatenate([zero_state, state_ssd_bwd[:, i].reshape(n_lat, SSM_INNER, SSM_STATE)], axis=0)
        yf, yb, s_f, s_b = ssd_scan(xbc_c, dt, ssm_a_log[i], h0_f, h0_b, seq_lens)
        j = i // 2
        router = moe_router[j] if i % 2 == 1 else None
        res = merge_branches(x, mod, a, yf, yb, xbc_c, z, p, gates, ssm_d[i, 0] + ssm_d[i, 1],
                             ssm_norm_w[i], w_branch_na[i], w_branch_ssm[i], w_branch_pool[i], w_out[i],
                             norm2_w[i], router, n_ctx_tok, lat_len)
        if i % 2 == 0:
            x1, h2 = res
            x = dense_ffn(h2, x1, mod, ffn_w1[j], ffn_w3[j], ffn_w2[j], n_ctx_tok, lat_len)
        else:
            x1, h2, logits = res
            x = moe_ffn(h2, logits, x1, mod, moe_w1[j], moe_w3[j], moe_w2[j], n_ctx_tok, lat_len)
        ks.append(k[:n_ctx_tok].reshape(n_ctx, ctx_len, NA_HEADS, NA_HEAD_DIM))
        vs.append(v[:n_ctx_tok].reshape(n_ctx, ctx_len, NA_HEADS, NA_HEAD_DIM))
        sfs.append(s_f[:n_ctx].reshape(n_ctx, SSM_HEADS, SSM_HEAD_DIM, SSM_STATE))
        sbs.append(s_b[:n_ctx].reshape(n_ctx, SSM_HEADS, SSM_HEAD_DIM, SSM_STATE))

    y_prompt = x[:n_ctx_tok].reshape(n_ctx, ctx_len, d)
    y_sample = x[n_ctx_tok:].reshape(n_lat, lat_len, d)
    return (y_prompt, y_sample, jnp.stack(ks, axis=1), jnp.stack(vs, axis=1),
            jnp.stack(sfs, axis=1), jnp.stack(sbs, axis=1))
```

```python
import functools
import math

import numpy as np
import jax
import jax.numpy as jnp
from jax import lax
from jax.experimental import pallas as pl
from jax.experimental.pallas import tpu as pltpu

F32 = jnp.float32
BF16 = jnp.bfloat16

D_MODEL = 1024
GRID_W = 64
NORM_EPS = 1e-6
N_BRANCH = 3
NA_HEADS = D_MODEL // 128
NA_HEAD_DIM = 64
NA_WIDTH = NA_HEADS * NA_HEAD_DIM
NA_KH = 8
NA_KW = 16
NA_SCALE = NA_HEAD_DIM ** -0.5
SSM_INNER = D_MODEL
SSM_HEAD_DIM = 64
SSM_HEADS = SSM_INNER // SSM_HEAD_DIM
SSM_GROUPS = 2
SSM_STATE = 128
SSM_CONV = 5
SSM_CONV_CH = SSM_INNER + 2 * SSM_GROUPS * SSM_STATE
POOL_WIDTH = D_MODEL // 2
POOL_WINDOWS = (2, 4, 8, 16)
POOL_GROUP = POOL_WIDTH // len(POOL_WINDOWS)
COL_Q = 0
COL_K = COL_Q + NA_WIDTH
COL_V = COL_K + NA_WIDTH
COL_Z = COL_V + NA_WIDTH
COL_XBC = COL_Z + SSM_INNER
COL_DT = COL_XBC + SSM_CONV_CH
COL_POOL = COL_DT + 2 * SSM_HEADS
IN_COLS = COL_POOL + POOL_WIDTH
N_EXPERTS = 8

LANES = 128
VMEM_LIMIT = 56 * 1024 * 1024

SSD_CHUNK = 128
SSD_BLOCK = 256
HALO = 16
LOCAL_TILE = 256
CONV_ROWS = 64
NEG = -1e30
LAT_Q_ROWS = 8
LAT_K_ROWS = 16
MOE_GROUP = 512
ROW_TILES = D_MODEL // LANES


def _cparams(sem, vmem=VMEM_LIMIT):
    return pltpu.CompilerParams(dimension_semantics=sem, vmem_limit_bytes=vmem)


def _const_spec(shape):
    nd = len(shape)
    return pl.BlockSpec(shape, lambda *_: (0,) * nd, pipeline_mode=pl.Buffered(1))


def _mod_row(i, tm, n_ctx_tok, lat_len):
    n_ctx_tiles = n_ctx_tok // tm
    return jnp.where(i < n_ctx_tiles, 0, 1 + (i - n_ctx_tiles) // (lat_len // tm))


def _silu(x):
    return x * jax.nn.sigmoid(x)


def _store_token_major(ref, val):
    rows = val.shape[0]
    for k in range(ROW_TILES):
        ref[pl.ds(k, rows, stride=ROW_TILES), :] = val[:, k * LANES:(k + 1) * LANES]


def _load_token_major(ref, rows, k):
    return ref[pl.ds(k, rows, stride=ROW_TILES), :]


def _split2(x):
    hi = x.astype(BF16)
    lo = (x - hi.astype(F32)).astype(BF16)
    return hi, lo


def _split3(x):
    hi = x.astype(BF16)
    r = x - hi.astype(F32)
    mid = r.astype(BF16)
    lo = (r - mid.astype(F32)).astype(BF16)
    return hi, mid, lo


def _ada_kernel(c_ref, w_ref, b_ref, o_ref):
    cb = _silu(c_ref[...]).astype(BF16)
    o_ref[...] = jnp.dot(cb, w_ref[...].astype(BF16), preferred_element_type=F32) + b_ref[...]


def ada_modulation(cond, ada_w, ada_b):
    depth, d, n = ada_w.shape
    r = cond.shape[0]
    tn = 1024
    return pl.pallas_call(
        _ada_kernel,
        out_shape=jax.ShapeDtypeStruct((depth, r, n), F32),
        grid=(depth, n // tn),
        in_specs=[pl.BlockSpec((r, d), lambda l, j: (0, 0)),
                  pl.BlockSpec((None, d, tn), lambda l, j: (l, 0, j)),
                  pl.BlockSpec((None, 1, tn), lambda l, j: (l, 0, j))],
        out_specs=pl.BlockSpec((None, r, tn), lambda l, j: (l, 0, j)),
        compiler_params=_cparams(("arbitrary", "arbitrary")),
        name="ada_modulation",
    )(cond, ada_w, ada_b.reshape(depth, 1, n))


def _modulated_norm(x, nw, shift, scale):
    y = x * lax.rsqrt(jnp.mean(x * x, axis=-1, keepdims=True) + NORM_EPS)
    return (y * nw) * (1.0 + scale) + shift


def _head_rmsnorm(acc, w_row, eblk):
    sq = (acc * acc).astype(BF16)
    half = eblk.shape[0]
    parts = [jnp.dot(sq[:, j * half:(j + 1) * half], eblk, preferred_element_type=F32)
             for j in range(NA_WIDTH // half)]
    ms = jnp.concatenate(parts, axis=-1) * (1.0 / NA_HEAD_DIM)
    return (acc * lax.rsqrt(ms + NORM_EPS)) * w_row


def _softplus(x):
    return jnp.maximum(x, 0.0) + jnp.log(1.0 + jnp.exp(-jnp.abs(x)))


def _proj_kernel(x_ref, mod_ref, nw_ref, wq_ref, wk_ref, wv_ref, wz_ref, wxbc_ref, wdt_ref, wu_ref,
                 qn_ref, kn_ref, dtb_ref, eblk_ref,
                 q_out, k_out, v_out, kb_out, vb_out, z_out, xbc_out, dt_out, u_out):
    h = _modulated_norm(x_ref[...], nw_ref[...], mod_ref[0:1, :], mod_ref[1:2, :])
    hb = h.astype(BF16)
    dot = lambda w_ref: jnp.dot(hb, w_ref[...], preferred_element_type=F32)
    eblk = eblk_ref[...]
    q = _head_rmsnorm(dot(wq_ref), qn_ref[...], eblk)
    q_out[...] = (q * NA_SCALE).astype(BF16)
    k = _head_rmsnorm(dot(wk_ref), kn_ref[...], eblk)
    k_out[...] = k
    kb_out[...] = k.astype(BF16)
    v = dot(wv_ref)
    v_out[...] = v
    vb_out[...] = v.astype(BF16)
    z_out[...] = dot(wz_ref).astype(BF16)
    xbc_out[...] = dot(wxbc_ref).astype(BF16)
    dt_out[...] = _softplus(dot(wdt_ref) + dtb_ref[...])
    u_out[...] = dot(wu_ref).astype(BF16)


def input_projection(x, mod, nw, w_in, qn, kn, dt_bias, n_ctx_tok, lat_len, tm=512):
    t, d = x.shape
    wb = w_in.astype(BF16)
    wq, wk, wv = wb[:, COL_Q:COL_K], wb[:, COL_K:COL_V], wb[:, COL_V:COL_Z]
    wz, wxbc, wu = wb[:, COL_Z:COL_XBC], wb[:, COL_XBC:COL_DT], wb[:, COL_POOL:]
    wdt = jnp.pad(wb[:, COL_DT:COL_POOL], ((0, 0), (0, LANES - 2 * SSM_HEADS)))
    dtb = jnp.pad(dt_bias.reshape(1, 2 * SSM_HEADS), ((0, 0), (0, LANES - 2 * SSM_HEADS)))
    qn_row = jnp.tile(qn, NA_HEADS).reshape(1, NA_WIDTH)
    kn_row = jnp.tile(kn, NA_HEADS).reshape(1, NA_WIDTH)
    half = 256
    eblk = jnp.asarray(np.kron(np.eye(half // NA_HEAD_DIM), np.ones((NA_HEAD_DIM, NA_HEAD_DIM))), BF16)
    row = lambda i: (i, 0)
    tok = lambda w, dt_: (pl.BlockSpec((tm, w), row), jax.ShapeDtypeStruct((t, w), dt_))
    outs = [tok(NA_WIDTH, BF16), tok(NA_WIDTH, F32), tok(NA_WIDTH, F32), tok(NA_WIDTH, BF16),
            tok(NA_WIDTH, BF16), tok(SSM_INNER, BF16), tok(SSM_CONV_CH, BF16), tok(LANES, F32),
            tok(POOL_WIDTH, BF16)]
    consts = [nw.reshape(1, d), wq, wk, wv, wz, wxbc, wdt, wu, qn_row, kn_row, dtb, eblk]
    return pl.pallas_call(
        _proj_kernel,
        out_shape=[o[1] for o in outs],
        grid=(t // tm,),
        in_specs=[pl.BlockSpec((tm, d), row),
                  pl.BlockSpec((None, 6, d), lambda i: (_mod_row(i, tm, n_ctx_tok, lat_len), 0, 0))]
                 + [_const_spec(c.shape) for c in consts],
        out_specs=[o[0] for o in outs],
        compiler_params=_cparams(("parallel",)),
        name="input_projection",
    )(x, mod, *consts)


def _gate_kernel(x_ref, mod_ref, nw_ref, wg_ref, bg_ref, g_out):
    h = _modulated_norm(x_ref[...], nw_ref[...], mod_ref[0:1, :], mod_ref[1:2, :])
    hb = h.astype(BF16)
    d = x_ref.shape[1]
    for j in range(N_BRANCH):
        sl = slice(j * d, (j + 1) * d)
        acc = jnp.dot(hb, wg_ref[:, sl], preferred_element_type=F32) + bg_ref[:, sl]
        g_out[:, sl] = jax.nn.sigmoid(acc).astype(BF16)


def branch_gates(x, mod, nw, w_gate, b_gate, n_ctx_tok, lat_len, tm=512):
    t, d = x.shape
    n = w_gate.shape[1]
    consts = [nw.reshape(1, d), w_gate.astype(BF16), b_gate.reshape(1, n)]
    return pl.pallas_call(
        _gate_kernel,
        out_shape=jax.ShapeDtypeStruct((t, n), BF16),
        grid=(t // tm,),
        in_specs=[pl.BlockSpec((tm, d), lambda i: (i, 0)),
                  pl.BlockSpec((None, 6, d), lambda i: (_mod_row(i, tm, n_ctx_tok, lat_len), 0, 0))]
                 + [_const_spec(c.shape) for c in consts],
        out_specs=pl.BlockSpec((tm, n), lambda i: (i, 0)),
        compiler_params=_cparams(("parallel",)),
        name="branch_gates",
    )(x, mod, *consts)


def _nt_dot(a, b):
    return lax.dot_general(a, b, (((1,), (1,)), ((), ())), preferred_element_type=F32)


def _head_masks(dtype):
    lane = lax.broadcasted_iota(jnp.int32, (1, LANES), 1)
    return [((lane >= j * NA_HEAD_DIM) & (lane < (j + 1) * NA_HEAD_DIM)).astype(dtype)
            for j in range(LANES // NA_HEAD_DIM)]


def _ctx_attn_kernel(q_ref, k_ref, v_ref, o_ref):
    q, k, v = q_ref[...], k_ref[...], v_ref[...]
    out = jnp.zeros(o_ref.shape, F32)
    for hm in _head_masks(BF16):
        s = _nt_dot(q * hm, k)
        p = jnp.exp(s - jnp.max(s, axis=-1, keepdims=True))
        den = jnp.sum(p, axis=-1, keepdims=True)
        out = out + jnp.dot(p.astype(BF16), v * hm, preferred_element_type=F32) / den
    o_ref[...] = out.astype(BF16)


def context_attention(q, kb, vb, n_seq, seq_len):
    blk = lambda: pl.BlockSpec((seq_len, LANES), lambda b, hp: (b, hp))
    return pl.pallas_call(
        _ctx_attn_kernel,
        out_shape=jax.ShapeDtypeStruct((n_seq * seq_len, NA_WIDTH), BF16),
        grid=(n_seq, NA_WIDTH // LANES),
        in_specs=[blk(), blk(), blk()],
        out_specs=blk(),
        compiler_params=_cparams(("parallel", "parallel")),
        name="context_attention",
    )(q, kb, vb)


def _lat_attn_kernel(q_ref, k_ref, v_ref, kc_ref, vc_ref, bias_ref, o_ref):
    rb = pl.program_id(1)
    n_rows = k_ref.shape[0] // GRID_W
    ks = jnp.clip(rb * LAT_Q_ROWS - NA_KH // 2, 0, n_rows - LAT_K_ROWS) * GRID_W
    ks = pl.multiple_of(ks, GRID_W)
    kw = k_ref[pl.ds(ks, LAT_K_ROWS * GRID_W), :]
    vw = v_ref[pl.ds(ks, LAT_K_ROWS * GRID_W), :]
    kc = kc_ref[...].astype(BF16)
    vc = vc_ref[...].astype(BF16)
    q = q_ref[...]
    out = jnp.zeros(o_ref.shape, F32)
    for j, hm in enumerate(_head_masks(BF16)):
        qh = q * hm
        s1 = _nt_dot(qh, kw) + bias_ref[j].astype(F32)
        s2 = _nt_dot(qh, kc)
        m = jnp.maximum(jnp.max(s1, axis=-1, keepdims=True), jnp.max(s2, axis=-1, keepdims=True))
        p1 = jnp.exp(s1 - m)
        p2 = jnp.exp(s2 - m)
        den = jnp.sum(p1, axis=-1, keepdims=True) + jnp.sum(p2, axis=-1, keepdims=True)
        o = (jnp.dot(p1.astype(BF16), vw * hm, preferred_element_type=F32)
             + jnp.dot(p2.astype(BF16), vc * hm, preferred_element_type=F32))
        out = out + o / den
    o_ref[...] = out.astype(BF16)


def _latent_bias_table(rpb, rows):
    col = np.arange(GRID_W)
    cs = np.clip(col - NA_KW // 2, 0, GRID_W - NA_KW)
    kc = np.arange(GRID_W)
    valid_col = (kc[None, :] >= cs[:, None]) & (kc[None, :] < cs[:, None] + NA_KW)
    dcol = np.clip(kc[None, :] - col[:, None] + NA_KW - 1, 0, 2 * NA_KW - 2)
    toe = jnp.where(valid_col[None, None], rpb[:, :, dcol], NEG)
    toe = jnp.concatenate([toe, jnp.full_like(toe[:, :1], NEG)], axis=1)
    n_d = 2 * NA_KH - 1
    tile_idx = np.full((3, LAT_Q_ROWS, LAT_K_ROWS), n_d, np.int32)
    for ty, r0 in enumerate((0, LAT_Q_ROWS, rows - LAT_Q_ROWS)):
        ks = int(np.clip(r0 - NA_KH // 2, 0, rows - LAT_K_ROWS))
        for qr in range(LAT_Q_ROWS):
            r = r0 + qr
            rs = int(np.clip(r - NA_KH // 2, 0, rows - NA_KH))
            for kr in range(LAT_K_ROWS):
                if rs <= ks + kr < rs + NA_KH:
                    tile_idx[ty, qr, kr] = ks + kr - r + NA_KH - 1
    onehot = np.zeros((3, LAT_Q_ROWS, LAT_K_ROWS, n_d + 1), np.float32)
    np.put_along_axis(onehot, tile_idx[..., None], 1.0, axis=-1)
    t = jnp.einsum('tqkd,hdxy->thqxky', jnp.asarray(onehot), toe)
    return t.reshape(3, NA_HEADS, LAT_Q_ROWS * GRID_W, LAT_K_ROWS * GRID_W).astype(BF16)


def latent_attention(q, kb, vb, k_ctx, v_ctx, rpb, n_ctx_tok, n_seq, seq_len):
    rows = seq_len // GRID_W
    n_rb = rows // LAT_Q_ROWS
    tq = LAT_Q_ROWS * GRID_W
    lc = k_ctx.shape[1]
    bias = _latent_bias_table(rpb, rows)
    hp_n = NA_WIDTH // LANES
    heads_per = LANES // NA_HEAD_DIM
    q0 = n_ctx_tok // tq
    s0 = n_ctx_tok // seq_len
    btype = lambda rb: jnp.where(rb == 0, 0, jnp.where(rb == n_rb - 1, 2, 1))
    return pl.pallas_call(
        _lat_attn_kernel,
        out_shape=jax.ShapeDtypeStruct((n_seq * seq_len, NA_WIDTH), BF16),
        grid=(hp_n, n_rb, n_seq),
        in_specs=[pl.BlockSpec((tq, LANES), lambda hp, rb, b: (q0 + b * n_rb + rb, hp)),
                  pl.BlockSpec((seq_len, LANES), lambda hp, rb, b: (s0 + b, hp)),
                  pl.BlockSpec((seq_len, LANES), lambda hp, rb, b: (s0 + b, hp)),
                  pl.BlockSpec((None, lc, LANES), lambda hp, rb, b: (b, 0, hp)),
                  pl.BlockSpec((None, lc, LANES), lambda hp, rb, b: (b, 0, hp)),
                  pl.BlockSpec((None, heads_per, tq, LAT_K_ROWS * GRID_W),
                               lambda hp, rb, b: (btype(rb), hp, 0, 0))],
        out_specs=pl.BlockSpec((tq, LANES), lambda hp, rb, b: (b * n_rb + rb, hp)),
        compiler_params=_cparams(("arbitrary", "arbitrary", "arbitrary")),
        name="latent_attention",
    )(q, kb, vb, k_ctx, v_ctx, bias)


def _local_kernel(seq_tiles_ref, xc_ref, xp_ref, xn_ref, uc_ref, up_ref, un_ref, cw_ref, cb_ref,
                  bandc_ref, bandp_ref, bandn_ref, pw_ref, ps_ref, xbc_out, p_out, ext_ref):
    i = pl.program_id(0)
    tl = xc_ref.shape[0]
    j = seq_tiles_ref[0, i]
    nt = seq_tiles_ref[1, i]
    has_prev = (j > 0).astype(F32)
    has_next = (j < nt - 1).astype(F32)

    ext_ref[0:HALO, :] = xp_ref[...].astype(F32) * has_prev
    ext_ref[HALO:HALO + tl, :] = xc_ref[...].astype(F32)
    ext_ref[HALO + tl:, :] = xn_ref[...].astype(F32) * has_next
    pad = SSM_CONV // 2
    for lt in range(xc_ref.shape[1] // LANES):
        ls = slice(lt * LANES, (lt + 1) * LANES)
        taps = [cw_ref[k:k + 1, ls] for k in range(SSM_CONV)]
        bias = cb_ref[:, ls]
        for rb in range(tl // CONV_ROWS):
            r0 = HALO + rb * CONV_ROWS - pad
            acc = bias + ext_ref[r0:r0 + CONV_ROWS, ls] * taps[0]
            for k in range(1, SSM_CONV):
                acc = acc + ext_ref[r0 + k:r0 + k + CONV_ROWS, ls] * taps[k]
            xbc_out[rb * CONV_ROWS:(rb + 1) * CONV_ROWS, ls] = _silu(acc).astype(BF16)

    t = j * tl + lax.broadcasted_iota(jnp.int32, (tl, 1), 0)
    seq_len = nt * tl
    uprev = (up_ref[...].astype(F32) * has_prev).astype(BF16)
    unext = (un_ref[...].astype(F32) * has_next).astype(BF16)
    for g, w in enumerate(POOL_WINDOWS):
        sl = slice(g * POOL_GROUP, (g + 1) * POOL_GROUP)
        ucur = uc_ref[:, sl]
        ssum = (jnp.dot(bandc_ref[g], ucur, preferred_element_type=F32)
                + jnp.dot(bandp_ref[g], uprev[:, sl], preferred_element_type=F32)
                + jnp.dot(bandn_ref[g], unext[:, sl], preferred_element_type=F32))
        lo = jnp.clip(t - w // 2, 0, seq_len)
        hi = jnp.clip(t - w // 2 + w, 0, seq_len)
        pooled = ssum / (hi - lo).astype(F32) - ucur.astype(F32)
        mixed = jnp.dot(pooled.astype(BF16), pw_ref[g], preferred_element_type=F32)
        p_out[:, sl] = (mixed * ps_ref[:, sl]).astype(BF16)


def _pool_bands(tl):
    t = np.arange(tl)[:, None]
    bc = np.zeros((len(POOL_WINDOWS), tl, tl), np.float32)
    bp = np.zeros((len(POOL_WINDOWS), tl, HALO), np.float32)
    bn = np.zeros((len(POOL_WINDOWS), tl, HALO), np.float32)
    for g, w in enumerate(POOL_WINDOWS):
        lo, hi = t - w // 2, t - w // 2 + w
        jc = np.arange(tl)[None, :]
        bc[g] = (jc >= lo) & (jc < hi)
        jp = np.arange(-HALO, 0)[None, :]
        bp[g] = (jp >= lo) & (jp < hi)
        jn = np.arange(tl, tl + HALO)[None, :]
        bn[g] = (jn >= lo) & (jn < hi)
    return jnp.asarray(bc, BF16), jnp.asarray(bp, BF16), jnp.asarray(bn, BF16)


def _seq_tile_table(seq_lens, tile):
    pos, cnt = [], []
    for sl in seq_lens:
        n = sl // tile
        pos += list(range(n))
        cnt += [n] * n
    return np.asarray([pos, cnt], np.int32)


def local_mixers(xbc, u, conv_w, conv_b, pool_w, pool_scale, seq_lens):
    t, c = xbc.shape
    tl = LOCAL_TILE
    table = jnp.asarray(_seq_tile_table(seq_lens, tl))
    hb = tl // HALO
    n_halo = t // HALO
    cur = lambda w: pl.BlockSpec((tl, w), lambda i, s: (i, 0))
    prev = lambda w: pl.BlockSpec((HALO, w), lambda i, s: (jnp.maximum(i * hb - 1, 0), 0))
    nxt = lambda w: pl.BlockSpec((HALO, w), lambda i, s: (jnp.minimum((i + 1) * hb, n_halo - 1), 0))
    cst = lambda a: pl.BlockSpec(a.shape, lambda i, s: (0,) * a.ndim, pipeline_mode=pl.Buffered(1))
    bc, bp, bn = _pool_bands(tl)
    consts = [conv_w, conv_b.reshape(1, c), bc, bp, bn, pool_w.astype(BF16), pool_scale.reshape(1, POOL_WIDTH)]
    return pl.pallas_call(
        _local_kernel,
        out_shape=[jax.ShapeDtypeStruct((t, c), BF16), jax.ShapeDtypeStruct((t, POOL_WIDTH), BF16)],
        grid_spec=pltpu.PrefetchScalarGridSpec(
            num_scalar_prefetch=1,
            grid=(t // tl,),
            in_specs=[cur(c), prev(c), nxt(c), cur(POOL_WIDTH), prev(POOL_WIDTH), nxt(POOL_WIDTH)]
                     + [cst(a) for a in consts],
            out_specs=[cur(c), cur(POOL_WIDTH)],
            scratch_shapes=[pltpu.VMEM((tl + 2 * HALO, c), F32)]),
        compiler_params=_cparams(("arbitrary",)),
        name="local_mixers",
    )(table, xbc, xbc, xbc, u, u, u, *consts)


def _ssd_direction(xbc_ref, dt_ref, alog_ref, expand_ref, state_ref, y_out, reverse):
    n_sub = xbc_ref.shape[0] // SSD_CHUNK
    for sub in (reversed(range(n_sub)) if reverse else range(n_sub)):
        rows = slice(sub * SSD_CHUNK, (sub + 1) * SSD_CHUNK)
        _ssd_chunk(xbc_ref, dt_ref, alog_ref, expand_ref, state_ref, y_out, rows, reverse)


def _ssd_chunk(xbc_ref, dt_ref, alog_ref, expand_ref, state_ref, y_out, rows, reverse):
    cl = SSD_CHUNK
    gw = SSM_INNER // SSM_GROUPS
    lane0 = SSM_HEADS if reverse else 0
    xs = xbc_ref[rows, :SSM_INNER].astype(F32)
    bm = xbc_ref[rows, SSM_INNER:SSM_INNER + SSM_GROUPS * SSM_STATE]
    cm = xbc_ref[rows, SSM_INNER + SSM_GROUPS * SSM_STATE:]
    dt = dt_ref[rows, :]
    a = dt * (-jnp.exp(alog_ref[...]))
    ri = lax.broadcasted_iota(jnp.int32, (cl, cl), 0)
    ci = lax.broadcasted_iota(jnp.int32, (cl, cl), 1)
    tri = (ci >= ri) if reverse else (ci <= ri)
    trib = tri.astype(BF16)
    acum = sum(jnp.dot(trib, part, preferred_element_type=F32) for part in _split3(a))
    acum_t = acum.T
    last = 0 if reverse else cl - 1
    atot = acum[last:last + 1, :]
    stacked = jnp.concatenate([dt, jnp.exp(acum), jnp.exp(atot - acum)], axis=0)
    ex = sum(jnp.dot(part, expand_ref[...], preferred_element_type=F32) for part in _split2(stacked))
    dt_e, eacum_e, wend_e = ex[:cl], ex[cl:2 * cl], ex[2 * cl:]
    xdt = xs * dt_e
    xdt_b = xdt.astype(BF16)
    lane = lax.broadcasted_iota(jnp.int32, (1, LANES), 1)
    left = lane < SSM_HEAD_DIM
    state = state_ref[...]
    state_b = state.astype(BF16)
    y_parts = []
    for g in range(SSM_GROUPS):
        bg = bm[:, g * SSM_STATE:(g + 1) * SSM_STATE]
        cg = cm[:, g * SSM_STATE:(g + 1) * SSM_STATE]
        cb = _nt_dot(cg, bg)
        y_off = jnp.dot(cg, state_b[:, g * gw:(g + 1) * gw], preferred_element_type=F32)
        heads_g = SSM_HEADS // SSM_GROUPS
        diag = []
        for pair in range(heads_g // 2):
            ms = []
            for hh in range(2):
                hl = lane0 + g * heads_g + pair * 2 + hh
                seg = acum[:, hl:hl + 1] - acum_t[hl:hl + 1, :]
                ms.append((cb * jnp.where(tri, jnp.exp(jnp.minimum(seg, 0.0)), 0.0)).astype(BF16))
            c0 = g * gw + pair * LANES
            xp = xdt_b[:, c0:c0 + LANES]
            rhs = jnp.concatenate([jnp.where(left, xp, jnp.zeros_like(xp)),
                                   jnp.where(left, jnp.zeros_like(xp), xp)], axis=0)
            diag.append(jnp.dot(jnp.concatenate(ms, axis=1), rhs, preferred_element_type=F32))
        y_parts.append(jnp.concatenate(diag, axis=1) + y_off * eacum_e[:, g * gw:(g + 1) * gw])
        xw = (xdt[:, g * gw:(g + 1) * gw] * wend_e[:, g * gw:(g + 1) * gw]).astype(BF16)
        upd = jnp.dot(bg.astype(F32).T.astype(BF16), xw, preferred_element_type=F32)
        state_ref[:, g * gw:(g + 1) * gw] = (state[:, g * gw:(g + 1) * gw]
                                             * eacum_e[last:last + 1, g * gw:(g + 1) * gw] + upd)
    y_out[rows, :] = jnp.concatenate(y_parts, axis=1).astype(y_out.dtype)


def _ssd_kernel(tab_ref, xf_ref, xb_ref, dtf_ref, dtb_ref, alog_ref, ef_ref, eb_ref, h0f_ref, h0b_ref,
                yf_out, yb_out, sf_out, sb_out, state_f, state_b):
    g = pl.program_id(0)

    @pl.when(tab_ref[1, g] == 1)
    def _():
        state_f[...] = h0f_ref[...].T
        state_b[...] = h0b_ref[...].T

    _ssd_direction(xf_ref, dtf_ref, alog_ref, ef_ref, state_f, yf_out, reverse=False)
    _ssd_direction(xb_ref, dtb_ref, alog_ref, eb_ref, state_b, yb_out, reverse=True)

    @pl.when(tab_ref[2, g] == 1)
    def _():
        sf_out[...] = state_f[...].T
        sb_out[...] = state_b[...].T


def _ssd_tables(seq_lens, cl):
    bwd, first, last, seq = [], [], [], []
    start = 0
    for s, sl in enumerate(seq_lens):
        nc = sl // cl
        for c in range(nc):
            bwd.append(start + nc - 1 - c)
            first.append(int(c == 0))
            last.append(int(c == nc - 1))
            seq.append(s)
        start += nc
    return np.asarray([bwd, first, last, seq], np.int32)


def ssd_scan(xbc, dt, a_log, h0_f, h0_b, seq_lens):
    t = xbc.shape[0]
    cl = SSD_BLOCK
    n_seq = len(seq_lens)
    tab = jnp.asarray(_ssd_tables(seq_lens, cl))
    alog = jnp.pad(a_log.reshape(1, 2 * SSM_HEADS), ((0, 0), (0, LANES - 2 * SSM_HEADS)))
    ef = np.zeros((LANES, SSM_INNER), np.float32)
    eb = np.zeros((LANES, SSM_INNER), np.float32)
    for h in range(SSM_HEADS):
        ef[h, h * SSM_HEAD_DIM:(h + 1) * SSM_HEAD_DIM] = 1.0
        eb[SSM_HEADS + h, h * SSM_HEAD_DIM:(h + 1) * SSM_HEAD_DIM] = 1.0
    fwd = lambda w: pl.BlockSpec((cl, w), lambda g, tb: (g, 0))
    bwd = lambda w: pl.BlockSpec((cl, w), lambda g, tb: (tb[0, g], 0))
    cst = lambda shape: pl.BlockSpec(shape, lambda g, tb: (0,) * len(shape), pipeline_mode=pl.Buffered(1))
    st = lambda: pl.BlockSpec((None, SSM_INNER, SSM_STATE), lambda g, tb: (tb[3, g], 0, 0))
    return pl.pallas_call(
        _ssd_kernel,
        out_shape=[jax.ShapeDtypeStruct((t, SSM_INNER), BF16), jax.ShapeDtypeStruct((t, SSM_INNER), BF16),
                   jax.ShapeDtypeStruct((n_seq, SSM_INNER, SSM_STATE), F32),
                   jax.ShapeDtypeStruct((n_seq, SSM_INNER, SSM_STATE), F32)],
        grid_spec=pltpu.PrefetchScalarGridSpec(
            num_scalar_prefetch=1,
            grid=(t // cl,),
            in_specs=[fwd(SSM_CONV_CH), bwd(SSM_CONV_CH), fwd(LANES), bwd(LANES),
                      cst((1, LANES)), cst((LANES, SSM_INNER)), cst((LANES, SSM_INNER)), st(), st()],
            out_specs=[fwd(SSM_INNER), bwd(SSM_INNER), st(), st()],
            scratch_shapes=[pltpu.VMEM((SSM_STATE, SSM_INNER), F32), pltpu.VMEM((SSM_STATE, SSM_INNER), F32)]),
        compiler_params=_cparams(("arbitrary",)),
        name="ssd_scan",
    )(tab, xbc, xbc, dt, dt, alog, jnp.asarray(ef, BF16), jnp.asarray(eb, BF16), h0_f, h0_b)


def _merge_kernel(x_ref, mod_ref, a_ref, yf_ref, yb_ref, xs_ref, z_ref, p_ref, g_ref,
                  dsum_ref, snw_ref, wna_ref, wssm_ref, wpool_ref, wout_ref, n2_ref, *rest,
                  with_router):
    if with_router:
        router_ref, x1_out, h2_out, logit_out = rest
    else:
        x1_out, h2_out = rest
    d = x_ref.shape[1]
    y = yf_ref[...].astype(F32) + yb_ref[...].astype(F32) + dsum_ref[...] * xs_ref[...].astype(F32)
    y = y * _silu(z_ref[...].astype(F32))
    gw = SSM_INNER // SSM_GROUPS
    parts = []
    for g in range(SSM_GROUPS):
        yg = y[:, g * gw:(g + 1) * gw]
        yg = yg * lax.rsqrt(jnp.mean(yg * yg, axis=-1, keepdims=True) + NORM_EPS)
        parts.append(yg * snw_ref[:, g * gw:(g + 1) * gw])
    s = jnp.concatenate(parts, axis=1).astype(BF16)
    gate = lambda j: g_ref[:, j * d:(j + 1) * d].astype(F32)
    merged = (gate(0) * jnp.dot(a_ref[...], wna_ref[...], preferred_element_type=F32)
              + gate(1) * jnp.dot(s, wssm_ref[...], preferred_element_type=F32)
              + gate(2) * jnp.dot(p_ref[...], wpool_ref[...], preferred_element_type=F32))
    x1 = x_ref[...] + mod_ref[2:3, :] * jnp.dot(merged.astype(BF16), wout_ref[...],
                                                preferred_element_type=F32)
    x1_out[...] = x1
    h2 = _modulated_norm(x1, n2_ref[...], mod_ref[3:4, :], mod_ref[4:5, :])
    if with_router:
        _store_token_major(h2_out, h2)
        hh, hl = _split2(h2)
        rh, rl = router_ref[0], router_ref[1]
        logit_out[...] = (jnp.dot(hh, rh, preferred_element_type=F32)
                          + jnp.dot(hl, rh, preferred_element_type=F32)
                          + jnp.dot(hh, rl, preferred_element_type=F32))
    else:
        h2_out[...] = h2.astype(BF16)


def merge_branches(x, mod, a, yf, yb, xbc, z, p, gates, d_sum, ssm_norm_w, wb_na, wb_ssm, wb_pool, w_out,
                   norm2_w, router, n_ctx_tok, lat_len, tm=512):
    t, d = x.shape
    row = lambda i: (i, 0)
    consts = [jnp.repeat(d_sum, SSM_HEAD_DIM).reshape(1, SSM_INNER), ssm_norm_w.reshape(1, SSM_INNER),
              wb_na.astype(BF16), wb_ssm.astype(BF16), wb_pool.astype(BF16), w_out.astype(BF16),
              norm2_w.reshape(1, d)]
    out_shape = [jax.ShapeDtypeStruct((t, d), F32)]
    out_specs = [pl.BlockSpec((tm, d), row)]
    with_router = router is not None
    if with_router:
        rpad = jnp.pad(router, ((0, 0), (0, LANES - router.shape[1])))
        rh = rpad.astype(BF16)
        rl = (rpad - rh.astype(F32)).astype(BF16)
        consts.append(jnp.stack([rh, rl]))
        out_shape += [jax.ShapeDtypeStruct((t * ROW_TILES, LANES), F32), jax.ShapeDtypeStruct((t, LANES), F32)]
        out_specs += [pl.BlockSpec((tm * ROW_TILES, LANES), row), pl.BlockSpec((tm, LANES), row)]
    else:
        out_shape += [jax.ShapeDtypeStruct((t, d), BF16)]
        out_specs += [pl.BlockSpec((tm, d), row)]
    return pl.pallas_call(
        functools.partial(_merge_kernel, with_router=with_router),
        out_shape=out_shape,
        grid=(t // tm,),
        in_specs=[pl.BlockSpec((tm, d), row),
                  pl.BlockSpec((None, 6, d), lambda i: (_mod_row(i, tm, n_ctx_tok, lat_len), 0, 0)),
                  pl.BlockSpec((tm, NA_WIDTH), row), pl.BlockSpec((tm, SSM_INNER), row),
                  pl.BlockSpec((tm, SSM_INNER), row), pl.BlockSpec((tm, SSM_INNER), row),
                  pl.BlockSpec((tm, SSM_INNER), row), pl.BlockSpec((tm, POOL_WIDTH), row),
                  pl.BlockSpec((tm, N_BRANCH * d), row)]
                 + [_const_spec(c.shape) for c in consts],
        out_specs=out_specs,
        compiler_params=_cparams(("parallel",)),
        name="merge_branches",
    )(x, mod, a, yf, yb, xbc, z, p, gates, *consts)


def _ffn_kernel(h_ref, x_ref, mod_ref, w1_ref, w3_ref, w2_ref, o_ref, acc_ref, *, tf):
    f = w1_ref.shape[1]
    hb = h_ref[...]
    for c in range(f // tf):
        sl = slice(c * tf, (c + 1) * tf)
        u = (_silu(jnp.dot(hb, w1_ref[:, sl], preferred_element_type=F32))
             * jnp.dot(hb, w3_ref[:, sl], preferred_element_type=F32))
        part = jnp.dot(u.astype(BF16), w2_ref[sl, :], preferred_element_type=F32)
        if c == 0:
            acc_ref[...] = part
        else:
            acc_ref[...] += part
    o_ref[...] = x_ref[...] + mod_ref[5:6, :] * acc_ref[...]


def dense_ffn(h2, x1, mod, w1, w3, w2, n_ctx_tok, lat_len, tm=512, tf=256):
    t, d = x1.shape
    consts = [w1.astype(BF16), w3.astype(BF16), w2.astype(BF16)]
    return pl.pallas_call(
        functools.partial(_ffn_kernel, tf=tf),
        out_shape=jax.ShapeDtypeStruct((t, d), F32),
        grid=(t // tm,),
        in_specs=[pl.BlockSpec((tm, d), lambda i: (i, 0)), pl.BlockSpec((tm, d), lambda i: (i, 0)),
                  pl.BlockSpec((None, 6, d), lambda i: (_mod_row(i, tm, n_ctx_tok, lat_len), 0, 0))]
                 + [_const_spec(c.shape) for c in consts],
        out_specs=pl.BlockSpec((tm, d), lambda i: (i, 0)),
        scratch_shapes=[pltpu.VMEM((tm, d), F32)],
        compiler_params=_cparams(("parallel",)),
        name="dense_ffn",
    )(h2, x1, mod, *consts)


def _route_kernel(logit_ref, info_out, gate_out, count_out, base_ref):
    i = pl.program_id(0)

    @pl.when(i == 0)
    def _():
        base_ref[...] = jnp.zeros_like(base_ref)

    lg = logit_ref[...]
    tm = lg.shape[0]
    lane = lax.broadcasted_iota(jnp.int32, lg.shape, 1)
    valid = lane < N_EXPERTS
    lg = jnp.where(valid, lg, -jnp.inf)
    m1 = jnp.max(lg, axis=-1, keepdims=True)
    e1 = jnp.min(jnp.where(lg == m1, lane, LANES), axis=-1, keepdims=True)
    lg2 = jnp.where(lane == e1, -jnp.inf, lg)
    m2 = jnp.max(lg2, axis=-1, keepdims=True)
    e2 = jnp.min(jnp.where(lg2 == m2, lane, LANES), axis=-1, keepdims=True)
    g1 = 1.0 / (1.0 + jnp.exp(m2 - m1))
    oh1 = (lane == e1).astype(F32)
    oh2 = (lane == e2).astype(F32)
    oh = oh1 + oh2
    ri = lax.broadcasted_iota(jnp.int32, (tm, tm), 0)
    ci = lax.broadcasted_iota(jnp.int32, (tm, tm), 1)
    before = jnp.dot((ci < ri).astype(BF16), oh.astype(BF16), preferred_element_type=F32) + base_ref[...]
    r1 = jnp.sum(oh1 * before, axis=-1, keepdims=True)
    r2 = jnp.sum(oh2 * before, axis=-1, keepdims=True)
    lane8 = lax.broadcasted_iota(jnp.int32, (tm, 8), 1)
    info = jnp.where(lane8 == 0, e1, jnp.where(lane8 == 1, e2,
                     jnp.where(lane8 == 2, r1.astype(jnp.int32), r2.astype(jnp.int32))))
    info_out[...] = info
    gate_out[...] = jnp.where(lane8 == 0, g1, 1.0 - g1)
    base_ref[...] += jnp.sum(oh, axis=0, keepdims=True)
    count_out[...] = base_ref[...].astype(jnp.int32)


def route_tokens(logits, tm=512):
    t = logits.shape[0]
    return pl.pallas_call(
        _route_kernel,
        out_shape=[jax.ShapeDtypeStruct((t, 8), jnp.int32), jax.ShapeDtypeStruct((t, 8), F32),
                   jax.ShapeDtypeStruct((1, LANES), jnp.int32)],
        grid=(t // tm,),
        in_specs=[pl.BlockSpec((tm, LANES), lambda i: (i, 0))],
        out_specs=[pl.BlockSpec((tm, 8), lambda i: (i, 0)), pl.BlockSpec((tm, 8), lambda i: (i, 0)),
                   pl.BlockSpec((1, LANES), lambda i: (0, 0))],
        scratch_shapes=[pltpu.VMEM((1, LANES), F32)],
        compiler_params=_cparams(("arbitrary",)),
        name="route_tokens",
    )(logits)


def _token_rows(idx):
    return pl.ds(pl.multiple_of(idx * ROW_TILES, ROW_TILES), ROW_TILES)


def _dma_loop(copy, n, wait):
    def body(i, c):
        if wait:
            copy(i).wait()
        else:
            copy(i).start()
        return c

    lax.fori_loop(0, n, body, 0, unroll=8)


def _expert_kernel(ge_ref, tcur_ref, tnext_ref, dprev_ref, h_ref, w1_ref, w3_ref, w2_ref, y_ref,
                   ibuf, obuf, xs_ref, acc_ref, gsem, ssem):
    g = pl.program_id(0)
    f = pl.program_id(1)
    n_groups = pl.num_programs(0) - 1
    tg = xs_ref.shape[0]
    live = g < n_groups

    def gather(tok_ref, slot, n):
        return pltpu.make_async_copy(h_ref.at[_token_rows(tok_ref[0, n])], ibuf.at[slot, _token_rows(n)],
                                     gsem.at[slot])

    def scatter(n):
        return pltpu.make_async_copy(obuf.at[_token_rows(n)], y_ref.at[_token_rows(dprev_ref[0, n])], ssem)

    def partial_out():
        xb = xs_ref[...]
        u = (_silu(jnp.dot(xb, w1_ref[...], preferred_element_type=F32))
             * jnp.dot(xb, w3_ref[...], preferred_element_type=F32))
        return jnp.dot(u.astype(BF16), w2_ref[...], preferred_element_type=F32)

    @pl.when((g == 0) & (f == 0))
    def _():
        obuf[...] = jnp.zeros_like(obuf)
        _dma_loop(lambda n: gather(tcur_ref, 0, n), tg, wait=False)

    @pl.when(live & (f == 0))
    def _():
        for n in range(tg):
            gather(tcur_ref, g % 2, n).wait()
        for k in range(ROW_TILES):
            xs_ref[:, k * LANES:(k + 1) * LANES] = _load_token_major(ibuf.at[g % 2], tg, k).astype(BF16)
        for n in range(tg):
            scatter(n).start()
        acc_ref[...] = partial_out()

    @pl.when(jnp.logical_not(live) & (f == 0))
    def _():
        _dma_loop(lambda n: gather(tcur_ref, g % 2, n), tg, wait=True)
        _dma_loop(scatter, tg, wait=False)

    @pl.when(live & (f == 1))
    def _():
        for n in range(tg):
            scatter(n).wait()
        for n in range(tg):
            gather(tnext_ref, (g + 1) % 2, n).start()
        _store_token_major(obuf, acc_ref[...] + partial_out())

    @pl.when(jnp.logical_not(live) & (f == 1))
    def _():
        _dma_loop(scatter, tg, wait=True)


def expert_ffn(h2, slot_tok, slot_dst, group_expert, n_rows_out, w1, w3, w2):
    n_slots = slot_tok.shape[0]
    d, ff = w1.shape[1], w1.shape[2]
    tf = ff // 2
    tg = MOE_GROUP
    ng = n_slots // tg
    tok_tiles = slot_tok.reshape(ng, 1, tg)
    spare = n_rows_out - tg + jnp.arange(tg, dtype=jnp.int32)
    dst_tiles = jnp.concatenate([spare, slot_dst]).reshape(ng + 1, 1, tg)
    smem = lambda imap: pl.BlockSpec((None, 1, tg), imap, memory_space=pltpu.SMEM)
    gi = lambda g: jnp.minimum(g, ng - 1)
    return pl.pallas_call(
        _expert_kernel,
        out_shape=jax.ShapeDtypeStruct((n_rows_out * ROW_TILES, LANES), F32),
        grid_spec=pltpu.PrefetchScalarGridSpec(
            num_scalar_prefetch=1,
            grid=(ng + 1, 2),
            in_specs=[smem(lambda g, f, ge: (gi(g), 0, 0)),
                      smem(lambda g, f, ge: (gi(g + 1), 0, 0)),
                      smem(lambda g, f, ge: (g, 0, 0)),
                      pl.BlockSpec(memory_space=pl.ANY),
                      pl.BlockSpec((None, d, tf), lambda g, f, ge: (ge[gi(g)], 0, f)),
                      pl.BlockSpec((None, d, tf), lambda g, f, ge: (ge[gi(g)], 0, f)),
                      pl.BlockSpec((None, tf, d), lambda g, f, ge: (ge[gi(g)], f, 0))],
            out_specs=pl.BlockSpec(memory_space=pl.ANY),
            scratch_shapes=[pltpu.VMEM((2, tg * ROW_TILES, LANES), F32), pltpu.VMEM((tg * ROW_TILES, LANES), F32),
                            pltpu.VMEM((tg, d), BF16), pltpu.VMEM((tg, d), F32),
                            pltpu.SemaphoreType.DMA((2,)), pltpu.SemaphoreType.DMA(())]),
        compiler_params=_cparams(("arbitrary", "arbitrary")),
        name="expert_ffn",
    )(group_expert, tok_tiles, tok_tiles, dst_tiles, h2, w1.astype(BF16), w3.astype(BF16), w2.astype(BF16))


def _combine_kernel(y_ref, x_ref, mod_ref, gate_ref, o_ref):
    tm = x_ref.shape[0]
    g0, g1 = gate_ref[:, 0:1], gate_ref[:, 1:2]
    for k in range(ROW_TILES):
        sl = slice(k * LANES, (k + 1) * LANES)
        y = (y_ref[pl.ds(k, tm, stride=2 * ROW_TILES), :] * g0
             + y_ref[pl.ds(ROW_TILES + k, tm, stride=2 * ROW_TILES), :] * g1)
        o_ref[:, sl] = x_ref[:, sl] + mod_ref[5:6, sl] * y


def combine_rows(y2, gates, x1, mod, n_ctx_tok, lat_len, tm=256):
    t, d = x1.shape
    return pl.pallas_call(
        _combine_kernel,
        out_shape=jax.ShapeDtypeStruct((t, d), F32),
        grid=(t // tm,),
        in_specs=[pl.BlockSpec((2 * tm * ROW_TILES, LANES), lambda i: (i, 0)),
                  pl.BlockSpec((tm, d), lambda i: (i, 0)),
                  pl.BlockSpec((None, 6, d), lambda i: (_mod_row(i, tm, n_ctx_tok, lat_len), 0, 0)),
                  pl.BlockSpec((tm, 8), lambda i: (i, 0))],
        out_specs=pl.BlockSpec((tm, d), lambda i: (i, 0)),
        compiler_params=_cparams(("parallel",)),
        name="moe_combine",
    )(y2, x1, mod, gates)


def moe_ffn(h2, logits, x1, mod, w1, w3, w2, n_ctx_tok, lat_len):
    t = x1.shape[0]
    info, gates, counts = route_tokens(logits)
    counts = counts[0, :N_EXPERTS]
    padded = (counts + MOE_GROUP - 1) // MOE_GROUP * MOE_GROUP
    pad_end = jnp.cumsum(padded)
    pad_start = pad_end - padded
    experts = info[:, 0:2]
    dest = jnp.sum(jnp.where(experts[..., None] == jnp.arange(N_EXPERTS), pad_start, 0), axis=-1) + info[:, 2:4]
    n_groups = 2 * t // MOE_GROUP + N_EXPERTS
    n_slots = n_groups * MOE_GROUP
    group_start = jnp.arange(n_groups, dtype=jnp.int32) * MOE_GROUP
    group_expert = jnp.minimum(jnp.sum(group_start[:, None] >= pad_end[None, :], axis=-1), N_EXPERTS - 1)
    slot_asg = jnp.full((n_slots,), -1, jnp.int32).at[dest.reshape(-1)].set(
        jnp.arange(2 * t, dtype=jnp.int32), unique_indices=True, mode="promise_in_bounds")
    is_pad = (slot_asg < 0).astype(jnp.int32)
    slot_dst = jnp.where(slot_asg >= 0, slot_asg, 2 * t - 1 + jnp.cumsum(is_pad))
    slot_tok = jnp.maximum(slot_asg, 0) // 2
    y2 = expert_ffn(h2, slot_tok, slot_dst, group_expert.astype(jnp.int32), n_slots + MOE_GROUP, w1, w3, w2)
    return combine_rows(y2, gates, x1, mod, n_ctx_tok, lat_len)


def kernel(x_prompt, x_sample, c, cache_na_k, cache_na_v, state_ssd_fwd, state_ssd_bwd, c_ctx, norm1_w, norm2_w, ada_w, ada_b, w_in, na_q_norm, na_k_norm, na_rpb, ssm_conv_w, ssm_conv_b, ssm_dt_bias, ssm_a_log, ssm_d, ssm_norm_w, pool_w, pool_scale, w_branch_na, w_branch_ssm, w_branch_pool, w_gate, b_gate, w_out, ffn_w1, ffn_w3, ffn_w2, moe_router, moe_w1, moe_w3, moe_w2):
    n_ctx, ctx_len, d = x_prompt.shape
    n_lat, lat_len, _ = x_sample.shape
    depth = ada_w.shape[0]
    n_ctx_tok = n_ctx * ctx_len
    seq_lens = (ctx_len,) * n_ctx + (lat_len,) * n_lat
    lc = cache_na_k.shape[2]

    x = jnp.concatenate([x_prompt.reshape(n_ctx_tok, d), x_sample.reshape(n_lat * lat_len, d)], axis=0)
    cond = jnp.concatenate([c_ctx[None, :], c], axis=0)
    cond = jnp.pad(cond, ((0, -cond.shape[0] % 8), (0, 0)))
    mods = ada_modulation(cond, ada_w, ada_b).reshape(depth, cond.shape[0], 6, d)
    zero_state = jnp.zeros((n_ctx, SSM_INNER, SSM_STATE), F32)

    ks, vs, sfs, sbs = [], [], [], []
    for i in range(depth):
        mod = mods[i]
        q, k, v, kb, vb, z, xbc, dt, u = input_projection(
            x, mod, norm1_w[i], w_in[i], na_q_norm[i], na_k_norm[i], ssm_dt_bias[i], n_ctx_tok, lat_len)
        gates = branch_gates(x, mod, norm1_w[i], w_gate[i], b_gate[i], n_ctx_tok, lat_len)
        a_ctx = context_attention(q, kb, vb, n_ctx, ctx_len)
        a_lat = latent_attention(q, kb, vb, cache_na_k[:, i].reshape(n_lat, lc, NA_WIDTH),
                                 cache_na_v[:, i].reshape(n_lat, lc, NA_WIDTH), na_rpb[i],
                                 n_ctx_tok, n_lat, lat_len)
        a = jnp.concatenate([a_ctx, a_lat], axis=0)
        xbc_c, p = local_mixers(xbc, u, ssm_conv_w[i], ssm_conv_b[i], pool_w[i], pool_scale[i], seq_lens)
        h0_f = jnp.concatenate([zero_state, state_ssd_fwd[:, i].reshape(n_lat, SSM_INNER, SSM_STATE)], axis=0)
        h0_b = jnp.concatenate([zero_state, state_ssd_bwd[:, i].reshape(n_lat, SSM_INNER, SSM_STATE)], axis=0)
        yf, yb, s_f, s_b = ssd_scan(xbc_c, dt, ssm_a_log[i], h0_f, h0_b, seq_lens)
        j = i // 2
        router = moe_router[j] if i % 2 == 1 else None
        res = merge_branches(x, mod, a, yf, yb, xbc_c, z, p, gates, ssm_d[i, 0] + ssm_d[i, 1],
                             ssm_norm_w[i], w_branch_na[i], w_branch_ssm[i], w_branch_pool[i], w_out[i],
                             norm2_w[i], router, n_ctx_tok, lat_len)
        if i % 2 == 0:
            x1, h2 = res
            x = dense_ffn(h2, x1, mod, ffn_w1[j], ffn_w3[j], ffn_w2[j], n_ctx_tok, lat_len)
        else:
            x1, h2, logits = res
            x = moe_ffn(h2, logits, x1, mod, moe_w1[j], moe_w3[j], moe_w2[j], n_ctx_tok, lat_len)
        ks.append(k[:n_ctx_tok].reshape(n_ctx, ctx_len, NA_HEADS, NA_HEAD_DIM))
        vs.append(v[:n_ctx_tok].reshape(n_ctx, ctx_len, NA_HEADS, NA_HEAD_DIM))
        sfs.append(s_f[:n_ctx].reshape(n_ctx, SSM_HEADS, SSM_HEAD_DIM, SSM_STATE))
        sbs.append(s_b[:n_ctx].reshape(n_ctx, SSM_HEADS, SSM_HEAD_DIM, SSM_STATE))

    y_prompt = x[:n_ctx_tok].reshape(n_ctx, ctx_len, d)
    y_sample = x[n_ctx_tok:].reshape(n_lat, lat_len, d)
    return (y_prompt, y_sample, jnp.stack(ks, axis=1), jnp.stack(vs, axis=1),
            jnp.stack(sfs, axis=1), jnp.stack(sbs, axis=1))
```

```python
import functools
import math

import numpy as np
import jax
import jax.numpy as jnp
from jax import lax
from jax.experimental import pallas as pl
from jax.experimental.pallas import tpu as pltpu

F32 = jnp.float32
BF16 = jnp.bfloat16

D_MODEL = 1024
GRID_W = 64
NORM_EPS = 1e-6
N_BRANCH = 3
NA_HEADS = D_MODEL // 128
NA_HEAD_DIM = 64
NA_WIDTH = NA_HEADS * NA_HEAD_DIM
NA_KH = 8
NA_KW = 16
NA_SCALE = NA_HEAD_DIM ** -0.5
SSM_INNER = D_MODEL
SSM_HEAD_DIM = 64
SSM_HEADS = SSM_INNER // SSM_HEAD_DIM
SSM_GROUPS = 2
SSM_STATE = 128
SSM_CONV = 5
SSM_CONV_CH = SSM_INNER + 2 * SSM_GROUPS * SSM_STATE
POOL_WIDTH = D_MODEL // 2
POOL_WINDOWS = (2, 4, 8, 16)
POOL_GROUP = POOL_WIDTH // len(POOL_WINDOWS)
COL_Q = 0
COL_K = COL_Q + NA_WIDTH
COL_V = COL_K + NA_WIDTH
COL_Z = COL_V + NA_WIDTH
COL_XBC = COL_Z + SSM_INNER
COL_DT = COL_XBC + SSM_CONV_CH
COL_POOL = COL_DT + 2 * SSM_HEADS
IN_COLS = COL_POOL + POOL_WIDTH
N_EXPERTS = 8

LANES = 128
VMEM_LIMIT = 56 * 1024 * 1024

SSD_CHUNK = 128
SSD_BLOCK = 256
HALO = 16
LOCAL_TILE = 256
CONV_ROWS = 64
NEG = -1e30
LAT_Q_ROWS = 8
LAT_K_ROWS = 16
MOE_GROUP = 512
ROW_TILES = D_MODEL // LANES


def _cparams(sem, vmem=VMEM_LIMIT):
    return pltpu.CompilerParams(dimension_semantics=sem, vmem_limit_bytes=vmem)


def _const_spec(shape):
    nd = len(shape)
    return pl.BlockSpec(shape, lambda *_: (0,) * nd, pipeline_mode=pl.Buffered(1))


def _mod_row(i, tm, n_ctx_tok, lat_len):
    n_ctx_tiles = n_ctx_tok // tm
    return jnp.where(i < n_ctx_tiles, 0, 1 + (i - n_ctx_tiles) // (lat_len // tm))


def _silu(x):
    return x * jax.nn.sigmoid(x)


def _store_token_major(ref, val):
    rows = val.shape[0]
    for k in range(ROW_TILES):
        ref[pl.ds(k, rows, stride=ROW_TILES), :] = val[:, k * LANES:(k + 1) * LANES]


def _load_token_major(ref, rows, k):
    return ref[pl.ds(k, rows, stride=ROW_TILES), :]


def _split2(x):
    hi = x.astype(BF16)
    lo = (x - hi.astype(F32)).astype(BF16)
    return hi, lo


def _split3(x):
    hi = x.astype(BF16)
    r = x - hi.astype(F32)
    mid = r.astype(BF16)
    lo = (r - mid.astype(F32)).astype(BF16)
    return hi, mid, lo


def _ada_kernel(c_ref, w_ref, b_ref, o_ref):
    cb = _silu(c_ref[...]).astype(BF16)
    o_ref[...] = jnp.dot(cb, w_ref[...].astype(BF16), preferred_element_type=F32) + b_ref[...]


def ada_modulation(cond, ada_w, ada_b):
    depth, d, n = ada_w.shape
    r = cond.shape[0]
    tn = 1024
    return pl.pallas_call(
        _ada_kernel,
        out_shape=jax.ShapeDtypeStruct((depth, r, n), F32),
        grid=(depth, n // tn),
        in_specs=[pl.BlockSpec((r, d), lambda l, j: (0, 0)),
                  pl.BlockSpec((None, d, tn), lambda l, j: (l, 0, j)),
                  pl.BlockSpec((None, 1, tn), lambda l, j: (l, 0, j))],
        out_specs=pl.BlockSpec((None, r, tn), lambda l, j: (l, 0, j)),
        compiler_params=_cparams(("arbitrary", "arbitrary")),
        name="ada_modulation",
    )(cond, ada_w, ada_b.reshape(depth, 1, n))


def _modulated_norm(x, nw, shift, scale):
    y = x * lax.rsqrt(jnp.mean(x * x, axis=-1, keepdims=True) + NORM_EPS)
    return (y * nw) * (1.0 + scale) + shift


def _head_rmsnorm(acc, w_row, eblk):
    sq = (acc * acc).astype(BF16)
    half = eblk.shape[0]
    parts = [jnp.dot(sq[:, j * half:(j + 1) * half], eblk, preferred_element_type=F32)
             for j in range(NA_WIDTH // half)]
    ms = jnp.concatenate(parts, axis=-1) * (1.0 / NA_HEAD_DIM)
    return (acc * lax.rsqrt(ms + NORM_EPS)) * w_row


def _softplus(x):
    return jnp.maximum(x, 0.0) + jnp.log(1.0 + jnp.exp(-jnp.abs(x)))


def _proj_kernel(x_ref, mod_ref, nw_ref, wq_ref, wk_ref, wv_ref, wz_ref, wxbc_ref, wdt_ref, wu_ref,
                 qn_ref, kn_ref, dtb_ref, eblk_ref,
                 q_out, k_out, v_out, kb_out, vb_out, z_out, xbc_out, dt_out, u_out):
    h = _modulated_norm(x_ref[...], nw_ref[...], mod_ref[0:1, :], mod_ref[1:2, :])
    hb = h.astype(BF16)
    dot = lambda w_ref: jnp.dot(hb, w_ref[...], preferred_element_type=F32)
    eblk = eblk_ref[...]
    q = _head_rmsnorm(dot(wq_ref), qn_ref[...], eblk)
    q_out[...] = (q * NA_SCALE).astype(BF16)
    k = _head_rmsnorm(dot(wk_ref), kn_ref[...], eblk)
    k_out[...] = k
    kb_out[...] = k.astype(BF16)
    v = dot(wv_ref)
    v_out[...] = v
    vb_out[...] = v.astype(BF16)
    z_out[...] = dot(wz_ref).astype(BF16)
    xbc_out[...] = dot(wxbc_ref).astype(BF16)
    dt_out[...] = _softplus(dot(wdt_ref) + dtb_ref[...])
    u_out[...] = dot(wu_ref).astype(BF16)


def input_projection(x, mod, nw, w_in, qn, kn, dt_bias, n_ctx_tok, lat_len, tm=512):
    t, d = x.shape
    wb = w_in.astype(BF16)
    wq, wk, wv = wb[:, COL_Q:COL_K], wb[:, COL_K:COL_V], wb[:, COL_V:COL_Z]
    wz, wxbc, wu = wb[:, COL_Z:COL_XBC], wb[:, COL_XBC:COL_DT], wb[:, COL_POOL:]
    wdt = jnp.pad(wb[:, COL_DT:COL_POOL], ((0, 0), (0, LANES - 2 * SSM_HEADS)))
    dtb = jnp.pad(dt_bias.reshape(1, 2 * SSM_HEADS), ((0, 0), (0, LANES - 2 * SSM_HEADS)))
    qn_row = jnp.tile(qn, NA_HEADS).reshape(1, NA_WIDTH)
    kn_row = jnp.tile(kn, NA_HEADS).reshape(1, NA_WIDTH)
    half = 256
    eblk = jnp.asarray(np.kron(np.eye(half // NA_HEAD_DIM), np.ones((NA_HEAD_DIM, NA_HEAD_DIM))), BF16)
    row = lambda i: (i, 0)
    tok = lambda w, dt_: (pl.BlockSpec((tm, w), row), jax.ShapeDtypeStruct((t, w), dt_))
    outs = [tok(NA_WIDTH, BF16), tok(NA_WIDTH, F32), tok(NA_WIDTH, F32), tok(NA_WIDTH, BF16),
            tok(NA_WIDTH, BF16), tok(SSM_INNER, BF16), tok(SSM_CONV_CH, BF16), tok(LANES, F32),
            tok(POOL_WIDTH, BF16)]
    consts = [nw.reshape(1, d), wq, wk, wv, wz, wxbc, wdt, wu, qn_row, kn_row, dtb, eblk]
    return pl.pallas_call(
        _proj_kernel,
        out_shape=[o[1] for o in outs],
        grid=(t // tm,),
        in_specs=[pl.BlockSpec((tm, d), row),
                  pl.BlockSpec((None, 6, d), lambda i: (_mod_row(i, tm, n_ctx_tok, lat_len), 0, 0))]
                 + [_const_spec(c.shape) for c in consts],
        out_specs=[o[0] for o in outs],
        compiler_params=_cparams(("parallel",)),
        name="input_projection",
    )(x, mod, *consts)


def _gate_kernel(x_ref, mod_ref, nw_ref, wg_ref, bg_ref, g_out):
    h = _modulated_norm(x_ref[...], nw_ref[...], mod_ref[0:1, :], mod_ref[1:2, :])
    hb = h.astype(BF16)
    d = x_ref.shape[1]
    for j in range(N_BRANCH):
        sl = slice(j * d, (j + 1) * d)
        acc = jnp.dot(hb, wg_ref[:, sl], preferred_element_type=F32) + bg_ref[:, sl]
        g_out[:, sl] = jax.nn.sigmoid(acc).astype(BF16)


def branch_gates(x, mod, nw, w_gate, b_gate, n_ctx_tok, lat_len, tm=512):
    t, d = x.shape
    n = w_gate.shape[1]
    consts = [nw.reshape(1, d), w_gate.astype(BF16), b_gate.reshape(1, n)]
    return pl.pallas_call(
        _gate_kernel,
        out_shape=jax.ShapeDtypeStruct((t, n), BF16),
        grid=(t // tm,),
        in_specs=[pl.BlockSpec((tm, d), lambda i: (i, 0)),
                  pl.BlockSpec((None, 6, d), lambda i: (_mod_row(i, tm, n_ctx_tok, lat_len), 0, 0))]
                 + [_const_spec(c.shape) for c in consts],
        out_specs=pl.BlockSpec((tm, n), lambda i: (i, 0)),
        compiler_params=_cparams(("parallel",)),
        name="branch_gates",
    )(x, mod, *consts)


def _nt_dot(a, b):
    return lax.dot_general(a, b, (((1,), (1,)), ((), ())), preferred_element_type=F32)


def _head_masks(dtype):
    lane = lax.broadcasted_iota(jnp.int32, (1, LANES), 1)
    return [((lane >= j * NA_HEAD_DIM) & (lane < (j + 1) * NA_HEAD_DIM)).astype(dtype)
            for j in range(LANES // NA_HEAD_DIM)]


def _ctx_attn_kernel(q_ref, k_ref, v_ref, o_ref):
    q, k, v = q_ref[...], k_ref[...], v_ref[...]
    out = jnp.zeros(o_ref.shape, F32)
    for hm in _head_masks(BF16):
        s = _nt_dot(q * hm, k)
        p = jnp.exp(s - jnp.max(s, axis=-1, keepdims=True))
        den = jnp.sum(p, axis=-1, keepdims=True)
        out = out + jnp.dot(p.astype(BF16), v * hm, preferred_element_type=F32) / den
    o_ref[...] = out.astype(BF16)


def context_attention(q, kb, vb, n_seq, seq_len):
    blk = lambda: pl.BlockSpec((seq_len, LANES), lambda b, hp: (b, hp))
    return pl.pallas_call(
        _ctx_attn_kernel,
        out_shape=jax.ShapeDtypeStruct((n_seq * seq_len, NA_WIDTH), BF16),
        grid=(n_seq, NA_WIDTH // LANES),
        in_specs=[blk(), blk(), blk()],
        out_specs=blk(),
        compiler_params=_cparams(("parallel", "parallel")),
        name="context_attention",
    )(q, kb, vb)


def _lat_attn_kernel(q_ref, k_ref, v_ref, kc_ref, vc_ref, bias_ref, o_ref):
    rb = pl.program_id(1)
    n_rows = k_ref.shape[0] // GRID_W
    ks = jnp.clip(rb * LAT_Q_ROWS - NA_KH // 2, 0, n_rows - LAT_K_ROWS) * GRID_W
    ks = pl.multiple_of(ks, GRID_W)
    kw = k_ref[pl.ds(ks, LAT_K_ROWS * GRID_W), :]
    vw = v_ref[pl.ds(ks, LAT_K_ROWS * GRID_W), :]
    kc = kc_ref[...].astype(BF16)
    vc = vc_ref[...].astype(BF16)
    q = q_ref[...]
    out = jnp.zeros(o_ref.shape, F32)
    for j, hm in enumerate(_head_masks(BF16)):
        qh = q * hm
        s1 = _nt_dot(qh, kw) + bias_ref[j].astype(F32)
        s2 = _nt_dot(qh, kc)
        m = jnp.maximum(jnp.max(s1, axis=-1, keepdims=True), jnp.max(s2, axis=-1, keepdims=True))
        p1 = jnp.exp(s1 - m)
        p2 = jnp.exp(s2 - m)
        den = jnp.sum(p1, axis=-1, keepdims=True) + jnp.sum(p2, axis=-1, keepdims=True)
        o = (jnp.dot(p1.astype(BF16), vw * hm, preferred_element_type=F32)
             + jnp.dot(p2.astype(BF16), vc * hm, preferred_element_type=F32))
        out = out + o / den
    o_ref[...] = out.astype(BF16)


def _latent_bias_table(rpb, rows):
    col = np.arange(GRID_W)
    cs = np.clip(col - NA_KW // 2, 0, GRID_W - NA_KW)
    kc = np.arange(GRID_W)
    valid_col = (kc[None, :] >= cs[:, None]) & (kc[None, :] < cs[:, None] + NA_KW)
    dcol = np.clip(kc[None, :] - col[:, None] + NA_KW - 1, 0, 2 * NA_KW - 2)
    toe = jnp.where(valid_col[None, None], rpb[:, :, dcol], NEG)
    toe = jnp.concatenate([toe, jnp.full_like(toe[:, :1], NEG)], axis=1)
    n_d = 2 * NA_KH - 1
    tile_idx = np.full((3, LAT_Q_ROWS, LAT_K_ROWS), n_d, np.int32)
    for ty, r0 in enumerate((0, LAT_Q_ROWS, rows - LAT_Q_ROWS)):
        ks = int(np.clip(r0 - NA_KH // 2, 0, rows - LAT_K_ROWS))
        for qr in range(LAT_Q_ROWS):
            r = r0 + qr
            rs = int(np.clip(r - NA_KH // 2, 0, rows - NA_KH))
            for kr in range(LAT_K_ROWS):
                if rs <= ks + kr < rs + NA_KH:
                    tile_idx[ty, qr, kr] = ks + kr - r + NA_KH - 1
    onehot = np.zeros((3, LAT_Q_ROWS, LAT_K_ROWS, n_d + 1), np.float32)
    np.put_along_axis(onehot, tile_idx[..., None], 1.0, axis=-1)
    t = jnp.einsum('tqkd,hdxy->thqxky', jnp.asarray(onehot), toe)
    return t.reshape(3, NA_HEADS, LAT_Q_ROWS * GRID_W, LAT_K_ROWS * GRID_W).astype(BF16)


def latent_attention(q, kb, vb, k_ctx, v_ctx, rpb, n_ctx_tok, n_seq, seq_len):
    rows = seq_len // GRID_W
    n_rb = rows // LAT_Q_ROWS
    tq = LAT_Q_ROWS * GRID_W
    lc = k_ctx.shape[1]
    bias = _latent_bias_table(rpb, rows)
    hp_n = NA_WIDTH // LANES
    heads_per = LANES // NA_HEAD_DIM
    q0 = n_ctx_tok // tq
    s0 = n_ctx_tok // seq_len
    btype = lambda rb: jnp.where(rb == 0, 0, jnp.where(rb == n_rb - 1, 2, 1))
    return pl.pallas_call(
        _lat_attn_kernel,
        out_shape=jax.ShapeDtypeStruct((n_seq * seq_len, NA_WIDTH), BF16),
        grid=(hp_n, n_rb, n_seq),
        in_specs=[pl.BlockSpec((tq, LANES), lambda hp, rb, b: (q0 + b * n_rb + rb, hp)),
                  pl.BlockSpec((seq_len, LANES), lambda hp, rb, b: (s0 + b, hp)),
                  pl.BlockSpec((seq_len, LANES), lambda hp, rb, b: (s0 + b, hp)),
                  pl.BlockSpec((None, lc, LANES), lambda hp, rb, b: (b, 0, hp)),
                  pl.BlockSpec((None, lc, LANES), lambda hp, rb, b: (b, 0, hp)),
                  pl.BlockSpec((None, heads_per, tq, LAT_K_ROWS * GRID_W),
                               lambda hp, rb, b: (btype(rb), hp, 0, 0))],
        out_specs=pl.BlockSpec((tq, LANES), lambda hp, rb, b: (b * n_rb + rb, hp)),
        compiler_params=_cparams(("arbitrary", "arbitrary", "arbitrary")),
        name="latent_attention",
    )(q, kb, vb, k_ctx, v_ctx, bias)


def _local_kernel(seq_tiles_ref, xc_ref, xp_ref, xn_ref, uc_ref, up_ref, un_ref, cw_ref, cb_ref,
                  bandc_ref, bandp_ref, bandn_ref, pw_ref, ps_ref, xbc_out, p_out, ext_ref):
    i = pl.program_id(0)
    tl = xc_ref.shape[0]
    j = seq_tiles_ref[0, i]
    nt = seq_tiles_ref[1, i]
    has_prev = (j > 0).astype(F32)
    has_next = (j < nt - 1).astype(F32)

    ext_ref[0:HALO, :] = xp_ref[...].astype(F32) * has_prev
    ext_ref[HALO:HALO + tl, :] = xc_ref[...].astype(F32)
    ext_ref[HALO + tl:, :] = xn_ref[...].astype(F32) * has_next
    pad = SSM_CONV // 2
    for lt in range(xc_ref.shape[1] // LANES):
        ls = slice(lt * LANES, (lt + 1) * LANES)
        taps = [cw_ref[k:k + 1, ls] for k in range(SSM_CONV)]
        bias = cb_ref[:, ls]
        for rb in range(tl // CONV_ROWS):
            r0 = HALO + rb * CONV_ROWS - pad
            acc = bias + ext_ref[r0:r0 + CONV_ROWS, ls] * taps[0]
            for k in range(1, SSM_CONV):
                acc = acc + ext_ref[r0 + k:r0 + k + CONV_ROWS, ls] * taps[k]
            xbc_out[rb * CONV_ROWS:(rb + 1) * CONV_ROWS, ls] = _silu(acc).astype(BF16)

    t = j * tl + lax.broadcasted_iota(jnp.int32, (tl, 1), 0)
    seq_len = nt * tl
    uprev = (up_ref[...].astype(F32) * has_prev).astype(BF16)
    unext = (un_ref[...].astype(F32) * has_next).astype(BF16)
    for g, w in enumerate(POOL_WINDOWS):
        sl = slice(g * POOL_GROUP, (g + 1) * POOL_GROUP)
        ucur = uc_ref[:, sl]
        ssum = (jnp.dot(bandc_ref[g], ucur, preferred_element_type=F32)
                + jnp.dot(bandp_ref[g], uprev[:, sl], preferred_element_type=F32)
                + jnp.dot(bandn_ref[g], unext[:, sl], preferred_element_type=F32))
        lo = jnp.clip(t - w // 2, 0, seq_len)
        hi = jnp.clip(t - w // 2 + w, 0, seq_len)
        pooled = ssum / (hi - lo).astype(F32) - ucur.astype(F32)
        mixed = jnp.dot(pooled.astype(BF16), pw_ref[g], preferred_element_type=F32)
        p_out[:, sl] = (mixed * ps_ref[:, sl]).astype(BF16)


def _pool_bands(tl):
    t = np.arange(tl)[:, None]
    bc = np.zeros((len(POOL_WINDOWS), tl, tl), np.float32)
    bp = np.zeros((len(POOL_WINDOWS), tl, HALO), np.float32)
    bn = np.zeros((len(POOL_WINDOWS), tl, HALO), np.float32)
    for g, w in enumerate(POOL_WINDOWS):
        lo, hi = t - w // 2, t - w // 2 + w
        jc = np.arange(tl)[None, :]
        bc[g] = (jc >= lo) & (jc < hi)
        jp = np.arange(-HALO, 0)[None, :]
        bp[g] = (jp >= lo) & (jp < hi)
        jn = np.arange(tl, tl + HALO)[None, :]
        bn[g] = (jn >= lo) & (jn < hi)
    return jnp.asarray(bc, BF16), jnp.asarray(bp, BF16), jnp.asarray(bn, BF16)


def _seq_tile_table(seq_lens, tile):
    pos, cnt = [], []
    for sl in seq_lens:
        n = sl // tile
        pos += list(range(n))
        cnt += [n] * n
    return np.asarray([pos, cnt], np.int32)


def local_mixers(xbc, u, conv_w, conv_b, pool_w, pool_scale, seq_lens):
    t, c = xbc.shape
    tl = LOCAL_TILE
    table = jnp.asarray(_seq_tile_table(seq_lens, tl))
    hb = tl // HALO
    n_halo = t // HALO
    cur = lambda w: pl.BlockSpec((tl, w), lambda i, s: (i, 0))
    prev = lambda w: pl.BlockSpec((HALO, w), lambda i, s: (jnp.maximum(i * hb - 1, 0), 0))
    nxt = lambda w: pl.BlockSpec((HALO, w), lambda i, s: (jnp.minimum((i + 1) * hb, n_halo - 1), 0))
    cst = lambda a: pl.BlockSpec(a.shape, lambda i, s: (0,) * a.ndim, pipeline_mode=pl.Buffered(1))
    bc, bp, bn = _pool_bands(tl)
    consts = [conv_w, conv_b.reshape(1, c), bc, bp, bn, pool_w.astype(BF16), pool_scale.reshape(1, POOL_WIDTH)]
    return pl.pallas_call(
        _local_kernel,
        out_shape=[jax.ShapeDtypeStruct((t, c), BF16), jax.ShapeDtypeStruct((t, POOL_WIDTH), BF16)],
        grid_spec=pltpu.PrefetchScalarGridSpec(
            num_scalar_prefetch=1,
            grid=(t // tl,),
            in_specs=[cur(c), prev(c), nxt(c), cur(POOL_WIDTH), prev(POOL_WIDTH), nxt(POOL_WIDTH)]
                     + [cst(a) for a in consts],
            out_specs=[cur(c), cur(POOL_WIDTH)],
            scratch_shapes=[pltpu.VMEM((tl + 2 * HALO, c), F32)]),
        compiler_params=_cparams(("arbitrary",)),
        name="local_mixers",
    )(table, xbc, xbc, xbc, u, u, u, *consts)


def _ssd_direction(xbc_ref, dt_ref, alog_ref, expand_ref, state_ref, y_out, reverse):
    n_sub = xbc_ref.shape[0] // SSD_CHUNK
    for sub in (reversed(range(n_sub)) if reverse else range(n_sub)):
        rows = slice(sub * SSD_CHUNK, (sub + 1) * SSD_CHUNK)
        _ssd_chunk(xbc_ref, dt_ref, alog_ref, expand_ref, state_ref, y_out, rows, reverse)


def _ssd_chunk(xbc_ref, dt_ref, alog_ref, expand_ref, state_ref, y_out, rows, reverse):
    cl = SSD_CHUNK
    gw = SSM_INNER // SSM_GROUPS
    lane0 = SSM_HEADS if reverse else 0
    xs = xbc_ref[rows, :SSM_INNER].astype(F32)
    bm = xbc_ref[rows, SSM_INNER:SSM_INNER + SSM_GROUPS * SSM_STATE]
    cm = xbc_ref[rows, SSM_INNER + SSM_GROUPS * SSM_STATE:]
    dt = dt_ref[rows, :]
    a = dt * (-jnp.exp(alog_ref[...]))
    ri = lax.broadcasted_iota(jnp.int32, (cl, cl), 0)
    ci = lax.broadcasted_iota(jnp.int32, (cl, cl), 1)
    tri = (ci >= ri) if reverse else (ci <= ri)
    trib = tri.astype(BF16)
    acum = sum(jnp.dot(trib, part, preferred_element_type=F32) for part in _split3(a))
    acum_t = acum.T
    last = 0 if reverse else cl - 1
    atot = acum[last:last + 1, :]
    stacked = jnp.concatenate([dt, jnp.exp(acum), jnp.exp(atot - acum)], axis=0)
    ex = sum(jnp.dot(part, expand_ref[...], preferred_element_type=F32) for part in _split2(stacked))
    dt_e, eacum_e, wend_e = ex[:cl], ex[cl:2 * cl], ex[2 * cl:]
    xdt = xs * dt_e
    xdt_b = xdt.astype(BF16)
    lane = lax.broadcasted_iota(jnp.int32, (1, LANES), 1)
    left = lane < SSM_HEAD_DIM
    state = state_ref[...]
    state_b = state.astype(BF16)
    y_parts = []
    for g in range(SSM_GROUPS):
        bg = bm[:, g * SSM_STATE:(g + 1) * SSM_STATE]
        cg = cm[:, g * SSM_STATE:(g + 1) * SSM_STATE]
        cb = _nt_dot(cg, bg)
        y_off = jnp.dot(cg, state_b[:, g * gw:(g + 1) * gw], preferred_element_type=F32)
        heads_g = SSM_HEADS // SSM_GROUPS
        diag = []
        for pair in range(heads_g // 2):
            ms = []
            for hh in range(2):
                hl = lane0 + g * heads_g + pair * 2 + hh
                seg = acum[:, hl:hl + 1] - acum_t[hl:hl + 1, :]
                ms.append((cb * jnp.where(tri, jnp.exp(jnp.minimum(seg, 0.0)), 0.0)).astype(BF16))
            c0 = g * gw + pair * LANES
            xp = xdt_b[:, c0:c0 + LANES]
            rhs = jnp.concatenate([jnp.where(left, xp, jnp.zeros_like(xp)),
                                   jnp.where(left, jnp.zeros_like(xp), xp)], axis=0)
            diag.append(jnp.dot(jnp.concatenate(ms, axis=1), rhs, preferred_element_type=F32))
        y_parts.append(jnp.concatenate(diag, axis=1) + y_off * eacum_e[:, g * gw:(g + 1) * gw])
        xw = (xdt[:, g * gw:(g + 1) * gw] * wend_e[:, g * gw:(g + 1) * gw]).astype(BF16)
        upd = jnp.dot(bg.astype(F32).T.astype(BF16), xw, preferred_element_type=F32)
        state_ref[:, g * gw:(g + 1) * gw] = (state[:, g * gw:(g + 1) * gw]
                                             * eacum_e[last:last + 1, g * gw:(g + 1) * gw] + upd)
    y_out[rows, :] = jnp.concatenate(y_parts, axis=1).astype(y_out.dtype)


def _ssd_kernel(tab_ref, xf_ref, xb_ref, dtf_ref, dtb_ref, alog_ref, ef_ref, eb_ref, h0f_ref, h0b_ref,
                yf_out, yb_out, sf_out, sb_out, state_f, state_b):
    g = pl.program_id(0)

    @pl.when(tab_ref[1, g] == 1)
    def _():
        state_f[...] = h0f_ref[...].T
        state_b[...] = h0b_ref[...].T

    _ssd_direction(xf_ref, dtf_ref, alog_ref, ef_ref, state_f, yf_out, reverse=False)
    _ssd_direction(xb_ref, dtb_ref, alog_ref, eb_ref, state_b, yb_out, reverse=True)

    @pl.when(tab_ref[2, g] == 1)
    def _():
        sf_out[...] = state_f[...].T
        sb_out[...] = state_b[...].T


def _ssd_tables(seq_lens, cl):
    bwd, first, last, seq = [], [], [], []
    start = 0
    for s, sl in enumerate(seq_lens):
        nc = sl // cl
        for c in range(nc):
            bwd.append(start + nc - 1 - c)
            first.append(int(c == 0))
            last.append(int(c == nc - 1))
            seq.append(s)
        start += nc
    return np.asarray([bwd, first, last, seq], np.int32)


def ssd_scan(xbc, dt, a_log, h0_f, h0_b, seq_lens):
    t = xbc.shape[0]
    cl = SSD_BLOCK
    n_seq = len(seq_lens)
    tab = jnp.asarray(_ssd_tables(seq_lens, cl))
    alog = jnp.pad(a_log.reshape(1, 2 * SSM_HEADS), ((0, 0), (0, LANES - 2 * SSM_HEADS)))
    ef = np.zeros((LANES, SSM_INNER), np.float32)
    eb = np.zeros((LANES, SSM_INNER), np.float32)
    for h in range(SSM_HEADS):
        ef[h, h * SSM_HEAD_DIM:(h + 1) * SSM_HEAD_DIM] = 1.0
        eb[SSM_HEADS + h, h * SSM_HEAD_DIM:(h + 1) * SSM_HEAD_DIM] = 1.0
    fwd = lambda w: pl.BlockSpec((cl, w), lambda g, tb: (g, 0))
    bwd = lambda w: pl.BlockSpec((cl, w), lambda g, tb: (tb[0, g], 0))
    cst = lambda shape: pl.BlockSpec(shape, lambda g, tb: (0,) * len(shape), pipeline_mode=pl.Buffered(1))
    st = lambda: pl.BlockSpec((None, SSM_INNER, SSM_STATE), lambda g, tb: (tb[3, g], 0, 0))
    return pl.pallas_call(
        _ssd_kernel,
        out_shape=[jax.ShapeDtypeStruct((t, SSM_INNER), BF16), jax.ShapeDtypeStruct((t, SSM_INNER), BF16),
                   jax.ShapeDtypeStruct((n_seq, SSM_INNER, SSM_STATE), F32),
                   jax.ShapeDtypeStruct((n_seq, SSM_INNER, SSM_STATE), F32)],
        grid_spec=pltpu.PrefetchScalarGridSpec(
            num_scalar_prefetch=1,
            grid=(t // cl,),
            in_specs=[fwd(SSM_CONV_CH), bwd(SSM_CONV_CH), fwd(LANES), bwd(LANES),
                      cst((1, LANES)), cst((LANES, SSM_INNER)), cst((LANES, SSM_INNER)), st(), st()],
            out_specs=[fwd(SSM_INNER), bwd(SSM_INNER), st(), st()],
            scratch_shapes=[pltpu.VMEM((SSM_STATE, SSM_INNER), F32), pltpu.VMEM((SSM_STATE, SSM_INNER), F32)]),
        compiler_params=_cparams(("arbitrary",)),
        name="ssd_scan",
    )(tab, xbc, xbc, dt, dt, alog, jnp.asarray(ef, BF16), jnp.asarray(eb, BF16), h0_f, h0_b)


def _merge_kernel(x_ref, mod_ref, a_ref, yf_ref, yb_ref, xs_ref, z_ref, p_ref, g_ref,
                  dsum_ref, snw_ref, wna_ref, wssm_ref, wpool_ref, wout_ref, n2_ref, *rest,
                  with_router):
    if with_router:
        router_ref, x1_out, h2_out, logit_out = rest
    else:
        x1_out, h2_out = rest
    d = x_ref.shape[1]
    y = yf_ref[...].astype(F32) + yb_ref[...].astype(F32) + dsum_ref[...] * xs_ref[...].astype(F32)
    y = y * _silu(z_ref[...].astype(F32))
    gw = SSM_INNER // SSM_GROUPS
    parts = []
    for g in range(SSM_GROUPS):
        yg = y[:, g * gw:(g + 1) * gw]
        yg = yg * lax.rsqrt(jnp.mean(yg * yg, axis=-1, keepdims=True) + NORM_EPS)
        parts.append(yg * snw_ref[:, g * gw:(g + 1) * gw])
    s = jnp.concatenate(parts, axis=1).astype(BF16)
    gate = lambda j: g_ref[:, j * d:(j + 1) * d].astype(F32)
    merged = (gate(0) * jnp.dot(a_ref[...], wna_ref[...], preferred_element_type=F32)
              + gate(1) * jnp.dot(s, wssm_ref[...], preferred_element_type=F32)
              + gate(2) * jnp.dot(p_ref[...], wpool_ref[...], preferred_element_type=F32))
    x1 = x_ref[...] + mod_ref[2:3, :] * jnp.dot(merged.astype(BF16), wout_ref[...],
                                                preferred_element_type=F32)
    x1_out[...] = x1
    h2 = _modulated_norm(x1, n2_ref[...], mod_ref[3:4, :], mod_ref[4:5, :])
    if with_router:
        _store_token_major(h2_out, h2)
        hh, hl = _split2(h2)
        rh, rl = router_ref[0], router_ref[1]
        logit_out[...] = (jnp.dot(hh, rh, preferred_element_type=F32)
                          + jnp.dot(hl, rh, preferred_element_type=F32)
                          + jnp.dot(hh, rl, preferred_element_type=F32))
    else:
        h2_out[...] = h2.astype(BF16)


def merge_branches(x, mod, a, yf, yb, xbc, z, p, gates, d_sum, ssm_norm_w, wb_na, wb_ssm, wb_pool, w_out,
                   norm2_w, router, n_ctx_tok, lat_len, tm=512):
    t, d = x.shape
    row = lambda i: (i, 0)
    consts = [jnp.repeat(d_sum, SSM_HEAD_DIM).reshape(1, SSM_INNER), ssm_norm_w.reshape(1, SSM_INNER),
              wb_na.astype(BF16), wb_ssm.astype(BF16), wb_pool.astype(BF16), w_out.astype(BF16),
              norm2_w.reshape(1, d)]
    out_shape = [jax.ShapeDtypeStruct((t, d), F32)]
    out_specs = [pl.BlockSpec((tm, d), row)]
    with_router = router is not None
    if with_router:
        rpad = jnp.pad(router, ((0, 0), (0, LANES - router.shape[1])))
        rh = rpad.astype(BF16)
        rl = (rpad - rh.astype(F32)).astype(BF16)
        consts.append(jnp.stack([rh, rl]))
        out_shape += [jax.ShapeDtypeStruct((t * ROW_TILES, LANES), F32), jax.ShapeDtypeStruct((t, LANES), F32)]
        out_specs += [pl.BlockSpec((tm * ROW_TILES, LANES), row), pl.BlockSpec((tm, LANES), row)]
    else:
        out_shape += [jax.ShapeDtypeStruct((t, d), BF16)]
        out_specs += [pl.BlockSpec((tm, d), row)]
    return pl.pallas_call(
        functools.partial(_merge_kernel, with_router=with_router),
        out_shape=out_shape,
        grid=(t // tm,),
        in_specs=[pl.BlockSpec((tm, d), row),
                  pl.BlockSpec((None, 6, d), lambda i: (_mod_row(i, tm, n_ctx_tok, lat_len), 0, 0)),
                  pl.BlockSpec((tm, NA_WIDTH), row), pl.BlockSpec((tm, SSM_INNER), row),
                  pl.BlockSpec((tm, SSM_INNER), row), pl.BlockSpec((tm, SSM_INNER), row),
                  pl.BlockSpec((tm, SSM_INNER), row), pl.BlockSpec((tm, POOL_WIDTH), row),
                  pl.BlockSpec((tm, N_BRANCH * d), row)]
                 + [_const_spec(c.shape) for c in consts],
        out_specs=out_specs,
        compiler_params=_cparams(("parallel",)),
        name="merge_branches",
    )(x, mod, a, yf, yb, xbc, z, p, gates, *consts)


def _ffn_kernel(h_ref, x_ref, mod_ref, w1_ref, w3_ref, w2_ref, o_ref, acc_ref, *, tf):
    f = w1_ref.shape[1]
    hb = h_ref[...]
    for c in range(f // tf):
        sl = slice(c * tf, (c + 1) * tf)
        u = (_silu(jnp.dot(hb, w1_ref[:, sl], preferred_element_type=F32))
             * jnp.dot(hb, w3_ref[:, sl], preferred_element_type=F32))
        part = jnp.dot(u.astype(BF16), w2_ref[sl, :], preferred_element_type=F32)
        if c == 0:
            acc_ref[...] = part
        else:
            acc_ref[...] += part
    o_ref[...] = x_ref[...] + mod_ref[5:6, :] * acc_ref[...]


def dense_ffn(h2, x1, mod, w1, w3, w2, n_ctx_tok, lat_len, tm=512, tf=256):
    t, d = x1.shape
    consts = [w1.astype(BF16), w3.astype(BF16), w2.astype(BF16)]
    return pl.pallas_call(
        functools.partial(_ffn_kernel, tf=tf),
        out_shape=jax.ShapeDtypeStruct((t, d), F32),
        grid=(t // tm,),
        in_specs=[pl.BlockSpec((tm, d), lambda i: (i, 0)), pl.BlockSpec((tm, d), lambda i: (i, 0)),
                  pl.BlockSpec((None, 6, d), lambda i: (_mod_row(i, tm, n_ctx_tok, lat_len), 0, 0))]
                 + [_const_spec(c.shape) for c in consts],
        out_specs=pl.BlockSpec((tm, d), lambda i: (i, 0)),
        scratch_shapes=[pltpu.VMEM((tm, d), F32)],
        compiler_params=_cparams(("parallel",)),
        name="dense_ffn",
    )(h2, x1, mod, *consts)


def _route_kernel(logit_ref, info_out, gate_out, count_out, base_ref):
    i = pl.program_id(0)

    @pl.when(i == 0)
    def _():
        base_ref[...] = jnp.zeros_like(base_ref)

    lg = logit_ref[...]
    tm = lg.shape[0]
    lane = lax.broadcasted_iota(jnp.int32, lg.shape, 1)
    valid = lane < N_EXPERTS
    lg = jnp.where(valid, lg, -jnp.inf)
    m1 = jnp.max(lg, axis=-1, keepdims=True)
    e1 = jnp.min(jnp.where(lg == m1, lane, LANES), axis=-1, keepdims=True)
    lg2 = jnp.where(lane == e1, -jnp.inf, lg)
    m2 = jnp.max(lg2, axis=-1, keepdims=True)
    e2 = jnp.min(jnp.where(lg2 == m2, lane, LANES), axis=-1, keepdims=True)
    g1 = 1.0 / (1.0 + jnp.exp(m2 - m1))
    oh1 = (lane == e1).astype(F32)
    oh2 = (lane == e2).astype(F32)
    oh = oh1 + oh2
    ri = lax.broadcasted_iota(jnp.int32, (tm, tm), 0)
    ci = lax.broadcasted_iota(jnp.int32, (tm, tm), 1)
    before = jnp.dot((ci < ri).astype(BF16), oh.astype(BF16), preferred_element_type=F32) + base_ref[...]
    r1 = jnp.sum(oh1 * before, axis=-1, keepdims=True)
    r2 = jnp.sum(oh2 * before, axis=-1, keepdims=True)
    lane8 = lax.broadcasted_iota(jnp.int32, (tm, 8), 1)
    info = jnp.where(lane8 == 0, e1, jnp.where(lane8 == 1, e2,
                     jnp.where(lane8 == 2, r1.astype(jnp.int32), r2.astype(jnp.int32))))
    info_out[...] = info
    gate_out[...] = jnp.where(lane8 == 0, g1, 1.0 - g1)
    base_ref[...] += jnp.sum(oh, axis=0, keepdims=True)
    count_out[...] = base_ref[...].astype(jnp.int32)


def route_tokens(logits, tm=512):
    t = logits.shape[0]
    return pl.pallas_call(
        _route_kernel,
        out_shape=[jax.ShapeDtypeStruct((t, 8), jnp.int32), jax.ShapeDtypeStruct((t, 8), F32),
                   jax.ShapeDtypeStruct((1, LANES), jnp.int32)],
        grid=(t // tm,),
        in_specs=[pl.BlockSpec((tm, LANES), lambda i: (i, 0))],
        out_specs=[pl.BlockSpec((tm, 8), lambda i: (i, 0)), pl.BlockSpec((tm, 8), lambda i: (i, 0)),
                   pl.BlockSpec((1, LANES), lambda i: (0, 0))],
        scratch_shapes=[pltpu.VMEM((1, LANES), F32)],
        compiler_params=_cparams(("arbitrary",)),
        name="route_tokens",
    )(logits)


def _token_rows(idx):
    return pl.ds(pl.multiple_of(idx * ROW_TILES, ROW_TILES), ROW_TILES)


def _dma_loop(copy, n, wait):
    def body(i, c):
        if wait:
            copy(i).wait()
        else:
            copy(i).start()
        return c

    lax.fori_loop(0, n, body, 0, unroll=8)


def _expert_kernel(ge_ref, tcur_ref, tnext_ref, dprev_ref, h_ref, w1_ref, w3_ref, w2_ref, y_ref,
                   ibuf, obuf, xs_ref, acc_ref, gsem, ssem):
    g = pl.program_id(0)
    f = pl.program_id(1)
    n_groups = pl.num_programs(0) - 1
    tg = xs_ref.shape[0]
    live = g < n_groups

    def gather(tok_ref, slot, n):
        return pltpu.make_async_copy(h_ref.at[_token_rows(tok_ref[0, n])], ibuf.at[slot, _token_rows(n)],
                                     gsem.at[slot])

    def scatter(n):
        return pltpu.make_async_copy(obuf.at[_token_rows(n)], y_ref.at[_token_rows(dprev_ref[0, n])], ssem)

    def partial_out():
        xb = xs_ref[...]
        u = (_silu(jnp.dot(xb, w1_ref[...], preferred_element_type=F32))
             * jnp.dot(xb, w3_ref[...], preferred_element_type=F32))
        return jnp.dot(u.astype(BF16), w2_ref[...], preferred_element_type=F32)

    @pl.when((g == 0) & (f == 0))
    def _():
        obuf[...] = jnp.zeros_like(obuf)
        _dma_loop(lambda n: gather(tcur_ref, 0, n), tg, wait=False)

    @pl.when(live & (f == 0))
    def _():
        for n in range(tg):
            gather(tcur_ref, g % 2, n).wait()
        for k in range(ROW_TILES):
            xs_ref[:, k * LANES:(k + 1) * LANES] = _load_token_major(ibuf.at[g % 2], tg, k).astype(BF16)
        for n in range(tg):
            scatter(n).start(priority=n % 2)
        acc_ref[...] = partial_out()

    @pl.when(jnp.logical_not(live) & (f == 0))
    def _():
        _dma_loop(lambda n: gather(tcur_ref, g % 2, n), tg, wait=True)
        _dma_loop(scatter, tg, wait=False)

    @pl.when(live & (f == 1))
    def _():
        for n in range(tg):
            scatter(n).wait()
        for n in range(tg):
            gather(tnext_ref, (g + 1) % 2, n).start(priority=1)
        _store_token_major(obuf, acc_ref[...] + partial_out())

    @pl.when(jnp.logical_not(live) & (f == 1))
    def _():
        _dma_loop(scatter, tg, wait=True)


def expert_ffn(h2, slot_tok, slot_dst, group_expert, n_rows_out, w1, w3, w2):
    n_slots = slot_tok.shape[0]
    d, ff = w1.shape[1], w1.shape[2]
    tf = ff // 2
    tg = MOE_GROUP
    ng = n_slots // tg
    tok_tiles = slot_tok.reshape(ng, 1, tg)
    spare = n_rows_out - tg + jnp.arange(tg, dtype=jnp.int32)
    dst_tiles = jnp.concatenate([spare, slot_dst]).reshape(ng + 1, 1, tg)
    smem = lambda imap: pl.BlockSpec((None, 1, tg), imap, memory_space=pltpu.SMEM)
    gi = lambda g: jnp.minimum(g, ng - 1)
    return pl.pallas_call(
        _expert_kernel,
        out_shape=jax.ShapeDtypeStruct((n_rows_out * ROW_TILES, LANES), F32),
        grid_spec=pltpu.PrefetchScalarGridSpec(
            num_scalar_prefetch=1,
            grid=(ng + 1, 2),
            in_specs=[smem(lambda g, f, ge: (gi(g), 0, 0)),
                      smem(lambda g, f, ge: (gi(g + 1), 0, 0)),
                      smem(lambda g, f, ge: (g, 0, 0)),
                      pl.BlockSpec(memory_space=pl.ANY),
                      pl.BlockSpec((None, d, tf), lambda g, f, ge: (ge[gi(g)], 0, f)),
                      pl.BlockSpec((None, d, tf), lambda g, f, ge: (ge[gi(g)], 0, f)),
                      pl.BlockSpec((None, tf, d), lambda g, f, ge: (ge[gi(g)], f, 0))],
            out_specs=pl.BlockSpec(memory_space=pl.ANY),
            scratch_shapes=[pltpu.VMEM((2, tg * ROW_TILES, LANES), F32), pltpu.VMEM((tg * ROW_TILES, LANES), F32),
                            pltpu.VMEM((tg, d), BF16), pltpu.VMEM((tg, d), F32),
                            pltpu.SemaphoreType.DMA((2,)), pltpu.SemaphoreType.DMA(())]),
        compiler_params=_cparams(("arbitrary", "arbitrary")),
        name="expert_ffn",
    )(group_expert, tok_tiles, tok_tiles, dst_tiles, h2, w1.astype(BF16), w3.astype(BF16), w2.astype(BF16))


def _combine_kernel(y_ref, x_ref, mod_ref, gate_ref, o_ref):
    tm = x_ref.shape[0]
    g0, g1 = gate_ref[:, 0:1], gate_ref[:, 1:2]
    for k in range(ROW_TILES):
        sl = slice(k * LANES, (k + 1) * LANES)
        y = (y_ref[pl.ds(k, tm, stride=2 * ROW_TILES), :] * g0
             + y_ref[pl.ds(ROW_TILES + k, tm, stride=2 * ROW_TILES), :] * g1)
        o_ref[:, sl] = x_ref[:, sl] + mod_ref[5:6, sl] * y


def combine_rows(y2, gates, x1, mod, n_ctx_tok, lat_len, tm=256):
    t, d = x1.shape
    return pl.pallas_call(
        _combine_kernel,
        out_shape=jax.ShapeDtypeStruct((t, d), F32),
        grid=(t // tm,),
        in_specs=[pl.BlockSpec((2 * tm * ROW_TILES, LANES), lambda i: (i, 0)),
                  pl.BlockSpec((tm, d), lambda i: (i, 0)),
                  pl.BlockSpec((None, 6, d), lambda i: (_mod_row(i, tm, n_ctx_tok, lat_len), 0, 0)),
                  pl.BlockSpec((tm, 8), lambda i: (i, 0))],
        out_specs=pl.BlockSpec((tm, d), lambda i: (i, 0)),
        compiler_params=_cparams(("parallel",)),
        name="moe_combine",
    )(y2, x1, mod, gates)


def moe_ffn(h2, logits, x1, mod, w1, w3, w2, n_ctx_tok, lat_len):
    t = x1.shape[0]
    info, gates, counts = route_tokens(logits)
    counts = counts[0, :N_EXPERTS]
    padded = (counts + MOE_GROUP - 1) // MOE_GROUP * MOE_GROUP
    pad_end = jnp.cumsum(padded)
    pad_start = pad_end - padded
    experts = info[:, 0:2]
    dest = jnp.sum(jnp.where(experts[..., None] == jnp.arange(N_EXPERTS), pad_start, 0), axis=-1) + info[:, 2:4]
    n_groups = 2 * t // MOE_GROUP + N_EXPERTS
    n_slots = n_groups * MOE_GROUP
    group_start = jnp.arange(n_groups, dtype=jnp.int32) * MOE_GROUP
    group_expert = jnp.minimum(jnp.sum(group_start[:, None] >= pad_end[None, :], axis=-1), N_EXPERTS - 1)
    slot_asg = jnp.full((n_slots,), -1, jnp.int32).at[dest.reshape(-1)].set(
        jnp.arange(2 * t, dtype=jnp.int32), unique_indices=True, mode="promise_in_bounds")
    is_pad = (slot_asg < 0).astype(jnp.int32)
    slot_dst = jnp.where(slot_asg >= 0, slot_asg, 2 * t - 1 + jnp.cumsum(is_pad))
    slot_tok = jnp.maximum(slot_asg, 0) // 2
    y2 = expert_ffn(h2, slot_tok, slot_dst, group_expert.astype(jnp.int32), n_slots + MOE_GROUP, w1, w3, w2)
    return combine_rows(y2, gates, x1, mod, n_ctx_tok, lat_len)


def kernel(x_prompt, x_sample, c, cache_na_k, cache_na_v, state_ssd_fwd, state_ssd_bwd, c_ctx, norm1_w, norm2_w, ada_w, ada_b, w_in, na_q_norm, na_k_norm, na_rpb, ssm_conv_w, ssm_conv_b, ssm_dt_bias, ssm_a_log, ssm_d, ssm_norm_w, pool_w, pool_scale, w_branch_na, w_branch_ssm, w_branch_pool, w_gate, b_gate, w_out, ffn_w1, ffn_w3, ffn_w2, moe_router, moe_w1, moe_w3, moe_w2):
    n_ctx, ctx_len, d = x_prompt.shape
    n_lat, lat_len, _ = x_sample.shape
    depth = ada_w.shape[0]
    n_ctx_tok = n_ctx * ctx_len
    seq_lens = (ctx_len,) * n_ctx + (lat_len,) * n_lat
    lc = cache_na_k.shape[2]

    x = jnp.concatenate([x_prompt.reshape(n_ctx_tok, d), x_sample.reshape(n_lat * lat_len, d)], axis=0)
    cond = jnp.concatenate([c_ctx[None, :], c], axis=0)
    cond = jnp.pad(cond, ((0, -cond.shape[0] % 8), (0, 0)))
    mods = ada_modulation(cond, ada_w, ada_b).reshape(depth, cond.shape[0], 6, d)
    zero_state = jnp.zeros((n_ctx, SSM_INNER, SSM_STATE), F32)

    ks, vs, sfs, sbs = [], [], [], []
    for i in range(depth):
        mod = mods[i]
        q, k, v, kb, vb, z, xbc, dt, u = input_projection(
            x, mod, norm1_w[i], w_in[i], na_q_norm[i], na_k_norm[i], ssm_dt_bias[i], n_ctx_tok, lat_len)
        gates = branch_gates(x, mod, norm1_w[i], w_gate[i], b_gate[i], n_ctx_tok, lat_len)
        a_ctx = context_attention(q, kb, vb, n_ctx, ctx_len)
        a_lat = latent_attention(q, kb, vb, cache_na_k[:, i].reshape(n_lat, lc, NA_WIDTH),
                                 cache_na_v[:, i].reshape(n_lat, lc, NA_WIDTH), na_rpb[i],
                                 n_ctx_tok, n_lat, lat_len)
        a = jnp.concatenate([a_ctx, a_lat], axis=0)
        xbc_c, p = local_mixers(xbc, u, ssm_conv_w[i], ssm_conv_b[i], pool_w[i], pool_scale[i], seq_lens)
        h0_f = jnp.concatenate([zero_state, state_ssd_fwd[:, i].reshape(n_lat, SSM_INNER, SSM_STATE)], axis=0)
        h0_b = jnp.concatenate([zero_state, state_ssd_bwd[:, i].reshape(n_lat, SSM_INNER, SSM_STATE)], axis=0)
        yf, yb, s_f, s_b = ssd_scan(xbc_c, dt, ssm_a_log[i], h0_f, h0_b, seq_lens)
        j = i // 2
        router = moe_router[j] if i % 2 == 1 else None
        res = merge_branches(x, mod, a, yf, yb, xbc_c, z, p, gates, ssm_d[i, 0] + ssm_d[i, 1],
                             ssm_norm_w[i], w_branch_na[i], w_branch_ssm[i], w_branch_pool[i], w_out[i],
                             norm2_w[i], router, n_ctx_tok, lat_len)
        if i % 2 == 0:
            x1, h2 = res
            x = dense_ffn(h2, x1, mod, ffn_w1[j], ffn_w3[j], ffn_w2[j], n_ctx_tok, lat_len)
        else:
            x1, h2, logits = res
            x = moe_ffn(h2, logits, x1, mod, moe_w1[j], moe_w3[j], moe_w2[j], n_ctx_tok, lat_len)
        ks.append(k[:n_ctx_tok].reshape(n_ctx, ctx_len, NA_HEADS, NA_HEAD_DIM))
        vs.append(v[:n_ctx_tok].reshape(n_ctx, ctx_len, NA_HEADS, NA_HEAD_DIM))
        sfs.append(s_f[:n_ctx].reshape(n_ctx, SSM_HEADS, SSM_HEAD_DIM, SSM_STATE))
        sbs.append(s_b[:n_ctx].reshape(n_ctx, SSM_HEADS, SSM_HEAD_DIM, SSM_STATE))

    y_prompt = x[:n_ctx_tok].reshape(n_ctx, ctx_len, d)
    y_sample = x[n_ctx_tok:].reshape(n_lat, lat_len, d)
    return (y_prompt, y_sample, jnp.stack(ks, axis=1), jnp.stack(vs, axis=1),
            jnp.stack(sfs, axis=1), jnp.stack(sbs, axis=1))
```

```python
import functools
import math

import numpy as np
import jax
import jax.numpy as jnp
from jax import lax
from jax.experimental import pallas as pl
from jax.experimental.pallas import tpu as pltpu

F32 = jnp.float32
BF16 = jnp.bfloat16

D_MODEL = 1024
GRID_W = 64
NORM_EPS = 1e-6
N_BRANCH = 3
NA_HEADS = D_MODEL // 128
NA_HEAD_DIM = 64
NA_WIDTH = NA_HEADS * NA_HEAD_DIM
NA_KH = 8
NA_KW = 16
NA_SCALE = NA_HEAD_DIM ** -0.5
SSM_INNER = D_MODEL
SSM_HEAD_DIM = 64
SSM_HEADS = SSM_INNER // SSM_HEAD_DIM
SSM_GROUPS = 2
SSM_STATE = 128
SSM_CONV = 5
SSM_CONV_CH = SSM_INNER + 2 * SSM_GROUPS * SSM_STATE
POOL_WIDTH = D_MODEL // 2
POOL_WINDOWS = (2, 4, 8, 16)
POOL_GROUP = POOL_WIDTH // len(POOL_WINDOWS)
COL_Q = 0
COL_K = COL_Q + NA_WIDTH
COL_V = COL_K + NA_WIDTH
COL_Z = COL_V + NA_WIDTH
COL_XBC = COL_Z + SSM_INNER
COL_DT = COL_XBC + SSM_CONV_CH
COL_POOL = COL_DT + 2 * SSM_HEADS
IN_COLS = COL_POOL + POOL_WIDTH
N_EXPERTS = 8

LANES = 128
VMEM_LIMIT = 56 * 1024 * 1024

SSD_CHUNK = 128
SSD_BLOCK = 256
HALO = 16
LOCAL_TILE = 256
CONV_ROWS = 64
NEG = -1e30
LAT_Q_ROWS = 8
LAT_K_ROWS = 16
MOE_GROUP = 512
ROW_TILES = D_MODEL // LANES


def _cparams(sem, vmem=VMEM_LIMIT):
    return pltpu.CompilerParams(dimension_semantics=sem, vmem_limit_bytes=vmem)


def _const_spec(shape):
    nd = len(shape)
    return pl.BlockSpec(shape, lambda *_: (0,) * nd, pipeline_mode=pl.Buffered(1))


def _mod_row(i, tm, n_ctx_tok, lat_len):
    n_ctx_tiles = n_ctx_tok // tm
    return jnp.where(i < n_ctx_tiles, 0, 1 + (i - n_ctx_tiles) // (lat_len // tm))


def _silu(x):
    return x * jax.nn.sigmoid(x)


def _store_token_major(ref, val):
    rows = val.shape[0]
    for k in range(ROW_TILES):
        ref[pl.ds(k, rows, stride=ROW_TILES), :] = val[:, k * LANES:(k + 1) * LANES]


def _load_token_major(ref, rows, k):
    return ref[pl.ds(k, rows, stride=ROW_TILES), :]


def _split2(x):
    hi = x.astype(BF16)
    lo = (x - hi.astype(F32)).astype(BF16)
    return hi, lo


def _split3(x):
    hi = x.astype(BF16)
    r = x - hi.astype(F32)
    mid = r.astype(BF16)
    lo = (r - mid.astype(F32)).astype(BF16)
    return hi, mid, lo


def _ada_kernel(c_ref, w_ref, b_ref, o_ref):
    cb = _silu(c_ref[...]).astype(BF16)
    o_ref[...] = jnp.dot(cb, w_ref[...].astype(BF16), preferred_element_type=F32) + b_ref[...]


def ada_modulation(cond, ada_w, ada_b):
    depth, d, n = ada_w.shape
    r = cond.shape[0]
    tn = 1024
    return pl.pallas_call(
        _ada_kernel,
        out_shape=jax.ShapeDtypeStruct((depth, r, n), F32),
        grid=(depth, n // tn),
        in_specs=[pl.BlockSpec((r, d), lambda l, j: (0, 0)),
                  pl.BlockSpec((None, d, tn), lambda l, j: (l, 0, j)),
                  pl.BlockSpec((None, 1, tn), lambda l, j: (l, 0, j))],
        out_specs=pl.BlockSpec((None, r, tn), lambda l, j: (l, 0, j)),
        compiler_params=_cparams(("arbitrary", "arbitrary")),
        name="ada_modulation",
    )(cond, ada_w, ada_b.reshape(depth, 1, n))


def _modulated_norm(x, nw, shift, scale):
    y = x * lax.rsqrt(jnp.mean(x * x, axis=-1, keepdims=True) + NORM_EPS)
    return (y * nw) * (1.0 + scale) + shift


def _head_rmsnorm(acc, w_row, eblk):
    sq = (acc * acc).astype(BF16)
    half = eblk.shape[0]
    parts = [jnp.dot(sq[:, j * half:(j + 1) * half], eblk, preferred_element_type=F32)
             for j in range(NA_WIDTH // half)]
    ms = jnp.concatenate(parts, axis=-1) * (1.0 / NA_HEAD_DIM)
    return (acc * lax.rsqrt(ms + NORM_EPS)) * w_row


def _softplus(x):
    return jnp.maximum(x, 0.0) + jnp.log(1.0 + jnp.exp(-jnp.abs(x)))


def _proj_kernel(x_ref, mod_ref, nw_ref, wq_ref, wk_ref, wv_ref, wz_ref, wxbc_ref, wdt_ref, wu_ref,
                 qn_ref, kn_ref, dtb_ref, eblk_ref,
                 q_out, k_out, v_out, kb_out, vb_out, z_out, xbc_out, dt_out, u_out):
    h = _modulated_norm(x_ref[...], nw_ref[...], mod_ref[0:1, :], mod_ref[1:2, :])
    hb = h.astype(BF16)
    dot = lambda w_ref: jnp.dot(hb, w_ref[...], preferred_element_type=F32)
    eblk = eblk_ref[...]
    q = _head_rmsnorm(dot(wq_ref), qn_ref[...], eblk)
    q_out[...] = (q * NA_SCALE).astype(BF16)
    k = _head_rmsnorm(dot(wk_ref), kn_ref[...], eblk)
    k_out[...] = k
    kb_out[...] = k.astype(BF16)
    v = dot(wv_ref)
    v_out[...] = v
    vb_out[...] = v.astype(BF16)
    z_out[...] = dot(wz_ref).astype(BF16)
    xbc_out[...] = dot(wxbc_ref).astype(BF16)
    dt_out[...] = _softplus(dot(wdt_ref) + dtb_ref[...])
    u_out[...] = dot(wu_ref).astype(BF16)


def input_projection(x, mod, nw, w_in, qn, kn, dt_bias, n_ctx_tok, lat_len, tm=512):
    t, d = x.shape
    wb = w_in.astype(BF16)
    wq, wk, wv = wb[:, COL_Q:COL_K], wb[:, COL_K:COL_V], wb[:, COL_V:COL_Z]
    wz, wxbc, wu = wb[:, COL_Z:COL_XBC], wb[:, COL_XBC:COL_DT], wb[:, COL_POOL:]
    wdt = jnp.pad(wb[:, COL_DT:COL_POOL], ((0, 0), (0, LANES - 2 * SSM_HEADS)))
    dtb = jnp.pad(dt_bias.reshape(1, 2 * SSM_HEADS), ((0, 0), (0, LANES - 2 * SSM_HEADS)))
    qn_row = jnp.tile(qn, NA_HEADS).reshape(1, NA_WIDTH)
    kn_row = jnp.tile(kn, NA_HEADS).reshape(1, NA_WIDTH)
    half = 256
    eblk = jnp.asarray(np.kron(np.eye(half // NA_HEAD_DIM), np.ones((NA_HEAD_DIM, NA_HEAD_DIM))), BF16)
    row = lambda i: (i, 0)
    tok = lambda w, dt_: (pl.BlockSpec((tm, w), row), jax.ShapeDtypeStruct((t, w), dt_))
    outs = [tok(NA_WIDTH, BF16), tok(NA_WIDTH, F32), tok(NA_WIDTH, F32), tok(NA_WIDTH, BF16),
            tok(NA_WIDTH, BF16), tok(SSM_INNER, BF16), tok(SSM_CONV_CH, BF16), tok(LANES, F32),
            tok(POOL_WIDTH, BF16)]
    consts = [nw.reshape(1, d), wq, wk, wv, wz, wxbc, wdt, wu, qn_row, kn_row, dtb, eblk]
    return pl.pallas_call(
        _proj_kernel,
        out_shape=[o[1] for o in outs],
        grid=(t // tm,),
        in_specs=[pl.BlockSpec((tm, d), row),
                  pl.BlockSpec((None, 6, d), lambda i: (_mod_row(i, tm, n_ctx_tok, lat_len), 0, 0))]
                 + [_const_spec(c.shape) for c in consts],
        out_specs=[o[0] for o in outs],
        compiler_params=_cparams(("parallel",)),
        name="input_projection",
    )(x, mod, *consts)


def _gate_kernel(x_ref, mod_ref, nw_ref, wg_ref, bg_ref, g_out):
    h = _modulated_norm(x_ref[...], nw_ref[...], mod_ref[0:1, :], mod_ref[1:2, :])
    hb = h.astype(BF16)
    d = x_ref.shape[1]
    for j in range(N_BRANCH):
        sl = slice(j * d, (j + 1) * d)
        acc = jnp.dot(hb, wg_ref[:, sl], preferred_element_type=F32) + bg_ref[:, sl]
        g_out[:, sl] = jax.nn.sigmoid(acc).astype(BF16)


def branch_gates(x, mod, nw, w_gate, b_gate, n_ctx_tok, lat_len, tm=512):
    t, d = x.shape
    n = w_gate.shape[1]
    consts = [nw.reshape(1, d), w_gate.astype(BF16), b_gate.reshape(1, n)]
    return pl.pallas_call(
        _gate_kernel,
        out_shape=jax.ShapeDtypeStruct((t, n), BF16),
        grid=(t // tm,),
        in_specs=[pl.BlockSpec((tm, d), lambda i: (i, 0)),
                  pl.BlockSpec((None, 6, d), lambda i: (_mod_row(i, tm, n_ctx_tok, lat_len), 0, 0))]
                 + [_const_spec(c.shape) for c in consts],
        out_specs=pl.BlockSpec((tm, n), lambda i: (i, 0)),
        compiler_params=_cparams(("parallel",)),
        name="branch_gates",
    )(x, mod, *consts)


def _nt_dot(a, b):
    return lax.dot_general(a, b, (((1,), (1,)), ((), ())), preferred_element_type=F32)


def _head_masks(dtype):
    lane = lax.broadcasted_iota(jnp.int32, (1, LANES), 1)
    return [((lane >= j * NA_HEAD_DIM) & (lane < (j + 1) * NA_HEAD_DIM)).astype(dtype)
            for j in range(LANES // NA_HEAD_DIM)]


def _ctx_attn_kernel(q_ref, k_ref, v_ref, o_ref):
    q, k, v = q_ref[...], k_ref[...], v_ref[...]
    out = jnp.zeros(o_ref.shape, F32)
    for hm in _head_masks(BF16):
        s = _nt_dot(q * hm, k)
        p = jnp.exp(s - jnp.max(s, axis=-1, keepdims=True))
        den = jnp.sum(p, axis=-1, keepdims=True)
        out = out + jnp.dot(p.astype(BF16), v * hm, preferred_element_type=F32) / den
    o_ref[...] = out.astype(BF16)


def context_attention(q, kb, vb, n_seq, seq_len):
    blk = lambda: pl.BlockSpec((seq_len, LANES), lambda b, hp: (b, hp))
    return pl.pallas_call(
        _ctx_attn_kernel,
        out_shape=jax.ShapeDtypeStruct((n_seq * seq_len, NA_WIDTH), BF16),
        grid=(n_seq, NA_WIDTH // LANES),
        in_specs=[blk(), blk(), blk()],
        out_specs=blk(),
        compiler_params=_cparams(("parallel", "parallel")),
        name="context_attention",
    )(q, kb, vb)


def _lat_attn_kernel(q_ref, k_ref, v_ref, kc_ref, vc_ref, bias_ref, o_ref):
    rb = pl.program_id(1)
    n_rows = k_ref.shape[0] // GRID_W
    ks = jnp.clip(rb * LAT_Q_ROWS - NA_KH // 2, 0, n_rows - LAT_K_ROWS) * GRID_W
    ks = pl.multiple_of(ks, GRID_W)
    kw = k_ref[pl.ds(ks, LAT_K_ROWS * GRID_W), :]
    vw = v_ref[pl.ds(ks, LAT_K_ROWS * GRID_W), :]
    kc = kc_ref[...].astype(BF16)
    vc = vc_ref[...].astype(BF16)
    q = q_ref[...]
    out = jnp.zeros(o_ref.shape, F32)
    for j, hm in enumerate(_head_masks(BF16)):
        qh = q * hm
        s1 = _nt_dot(qh, kw) + bias_ref[j].astype(F32)
        s2 = _nt_dot(qh, kc)
        m = jnp.maximum(jnp.max(s1, axis=-1, keepdims=True), jnp.max(s2, axis=-1, keepdims=True))
        p1 = jnp.exp(s1 - m)
        p2 = jnp.exp(s2 - m)
        den = jnp.sum(p1, axis=-1, keepdims=True) + jnp.sum(p2, axis=-1, keepdims=True)
        o = (jnp.dot(p1.astype(BF16), vw * hm, preferred_element_type=F32)
             + jnp.dot(p2.astype(BF16), vc * hm, preferred_element_type=F32))
        out = out + o / den
    o_ref[...] = out.astype(BF16)


def _latent_bias_table(rpb, rows):
    col = np.arange(GRID_W)
    cs = np.clip(col - NA_KW // 2, 0, GRID_W - NA_KW)
    kc = np.arange(GRID_W)
    valid_col = (kc[None, :] >= cs[:, None]) & (kc[None, :] < cs[:, None] + NA_KW)
    dcol = np.clip(kc[None, :] - col[:, None] + NA_KW - 1, 0, 2 * NA_KW - 2)
    toe = jnp.where(valid_col[None, None], rpb[:, :, dcol], NEG)
    toe = jnp.concatenate([toe, jnp.full_like(toe[:, :1], NEG)], axis=1)
    n_d = 2 * NA_KH - 1
    tile_idx = np.full((3, LAT_Q_ROWS, LAT_K_ROWS), n_d, np.int32)
    for ty, r0 in enumerate((0, LAT_Q_ROWS, rows - LAT_Q_ROWS)):
        ks = int(np.clip(r0 - NA_KH // 2, 0, rows - LAT_K_ROWS))
        for qr in range(LAT_Q_ROWS):
            r = r0 + qr
            rs = int(np.clip(r - NA_KH // 2, 0, rows - NA_KH))
            for kr in range(LAT_K_ROWS):
                if rs <= ks + kr < rs + NA_KH:
                    tile_idx[ty, qr, kr] = ks + kr - r + NA_KH - 1
    onehot = np.zeros((3, LAT_Q_ROWS, LAT_K_ROWS, n_d + 1), np.float32)
    np.put_along_axis(onehot, tile_idx[..., None], 1.0, axis=-1)
    t = jnp.einsum('tqkd,hdxy->thqxky', jnp.asarray(onehot), toe)
    return t.reshape(3, NA_HEADS, LAT_Q_ROWS * GRID_W, LAT_K_ROWS * GRID_W).astype(BF16)


def latent_attention(q, kb, vb, k_ctx, v_ctx, rpb, n_ctx_tok, n_seq, seq_len):
    rows = seq_len // GRID_W
    n_rb = rows // LAT_Q_ROWS
    tq = LAT_Q_ROWS * GRID_W
    lc = k_ctx.shape[1]
    bias = _latent_bias_table(rpb, rows)
    hp_n = NA_WIDTH // LANES
    heads_per = LANES // NA_HEAD_DIM
    q0 = n_ctx_tok // tq
    s0 = n_ctx_tok // seq_len
    btype = lambda rb: jnp.where(rb == 0, 0, jnp.where(rb == n_rb - 1, 2, 1))
    return pl.pallas_call(
        _lat_attn_kernel,
        out_shape=jax.ShapeDtypeStruct((n_seq * seq_len, NA_WIDTH), BF16),
        grid=(hp_n, n_rb, n_seq),
        in_specs=[pl.BlockSpec((tq, LANES), lambda hp, rb, b: (q0 + b * n_rb + rb, hp)),
                  pl.BlockSpec((seq_len, LANES), lambda hp, rb, b: (s0 + b, hp)),
                  pl.BlockSpec((seq_len, LANES), lambda hp, rb, b: (s0 + b, hp)),
                  pl.BlockSpec((None, lc, LANES), lambda hp, rb, b: (b, 0, hp)),
                  pl.BlockSpec((None, lc, LANES), lambda hp, rb, b: (b, 0, hp)),
                  pl.BlockSpec((None, heads_per, tq, LAT_K_ROWS * GRID_W),
                               lambda hp, rb, b: (btype(rb), hp, 0, 0))],
        out_specs=pl.BlockSpec((tq, LANES), lambda hp, rb, b: (b * n_rb + rb, hp)),
        compiler_params=_cparams(("arbitrary", "arbitrary", "arbitrary")),
        name="latent_attention",
    )(q, kb, vb, k_ctx, v_ctx, bias)


def _local_kernel(seq_tiles_ref, xc_ref, xp_ref, xn_ref, uc_ref, up_ref, un_ref, cw_ref, cb_ref,
                  bandc_ref, bandp_ref, bandn_ref, pw_ref, ps_ref, xbc_out, p_out, ext_ref):
    i = pl.program_id(0)
    tl = xc_ref.shape[0]
    j = seq_tiles_ref[0, i]
    nt = seq_tiles_ref[1, i]
    has_prev = (j > 0).astype(F32)
    has_next = (j < nt - 1).astype(F32)

    ext_ref[0:HALO, :] = xp_ref[...].astype(F32) * has_prev
    ext_ref[HALO:HALO + tl, :] = xc_ref[...].astype(F32)
    ext_ref[HALO + tl:, :] = xn_ref[...].astype(F32) * has_next
    pad = SSM_CONV // 2
    for lt in range(xc_ref.shape[1] // LANES):
        ls = slice(lt * LANES, (lt + 1) * LANES)
        taps = [cw_ref[k:k + 1, ls] for k in range(SSM_CONV)]
        bias = cb_ref[:, ls]
        for rb in range(tl // CONV_ROWS):
            r0 = HALO + rb * CONV_ROWS - pad
            acc = bias + ext_ref[r0:r0 + CONV_ROWS, ls] * taps[0]
            for k in range(1, SSM_CONV):
                acc = acc + ext_ref[r0 + k:r0 + k + CONV_ROWS, ls] * taps[k]
            xbc_out[rb * CONV_ROWS:(rb + 1) * CONV_ROWS, ls] = _silu(acc).astype(BF16)

    t = j * tl + lax.broadcasted_iota(jnp.int32, (tl, 1), 0)
    seq_len = nt * tl
    uprev = (up_ref[...].astype(F32) * has_prev).astype(BF16)
    unext = (un_ref[...].astype(F32) * has_next).astype(BF16)
    for g, w in enumerate(POOL_WINDOWS):
        sl = slice(g * POOL_GROUP, (g + 1) * POOL_GROUP)
        ucur = uc_ref[:, sl]
        ssum = (jnp.dot(bandc_ref[g], ucur, preferred_element_type=F32)
                + jnp.dot(bandp_ref[g], uprev[:, sl], preferred_element_type=F32)
                + jnp.dot(bandn_ref[g], unext[:, sl], preferred_element_type=F32))
        lo = jnp.clip(t - w // 2, 0, seq_len)
        hi = jnp.clip(t - w // 2 + w, 0, seq_len)
        pooled = ssum / (hi - lo).astype(F32) - ucur.astype(F32)
        mixed = jnp.dot(pooled.astype(BF16), pw_ref[g], preferred_element_type=F32)
        p_out[:, sl] = (mixed * ps_ref[:, sl]).astype(BF16)


def _pool_bands(tl):
    t = np.arange(tl)[:, None]
    bc = np.zeros((len(POOL_WINDOWS), tl, tl), np.float32)
    bp = np.zeros((len(POOL_WINDOWS), tl, HALO), np.float32)
    bn = np.zeros((len(POOL_WINDOWS), tl, HALO), np.float32)
    for g, w in enumerate(POOL_WINDOWS):
        lo, hi = t - w // 2, t - w // 2 + w
        jc = np.arange(tl)[None, :]
        bc[g] = (jc >= lo) & (jc < hi)
        jp = np.arange(-HALO, 0)[None, :]
        bp[g] = (jp >= lo) & (jp < hi)
        jn = np.arange(tl, tl + HALO)[None, :]
        bn[g] = (jn >= lo) & (jn < hi)
    return jnp.asarray(bc, BF16), jnp.asarray(bp, BF16), jnp.asarray(bn, BF16)


def _seq_tile_table(seq_lens, tile):
    pos, cnt = [], []
    for sl in seq_lens:
        n = sl // tile
        pos += list(range(n))
        cnt += [n] * n
    return np.asarray([pos, cnt], np.int32)


def local_mixers(xbc, u, conv_w, conv_b, pool_w, pool_scale, seq_lens):
    t, c = xbc.shape
    tl = LOCAL_TILE
    table = jnp.asarray(_seq_tile_table(seq_lens, tl))
    hb = tl // HALO
    n_halo = t // HALO
    cur = lambda w: pl.BlockSpec((tl, w), lambda i, s: (i, 0))
    prev = lambda w: pl.BlockSpec((HALO, w), lambda i, s: (jnp.maximum(i * hb - 1, 0), 0))
    nxt = lambda w: pl.BlockSpec((HALO, w), lambda i, s: (jnp.minimum((i + 1) * hb, n_halo - 1), 0))
    cst = lambda a: pl.BlockSpec(a.shape, lambda i, s: (0,) * a.ndim, pipeline_mode=pl.Buffered(1))
    bc, bp, bn = _pool_bands(tl)
    consts = [conv_w, conv_b.reshape(1, c), bc, bp, bn, pool_w.astype(BF16), pool_scale.reshape(1, POOL_WIDTH)]
    return pl.pallas_call(
        _local_kernel,
        out_shape=[jax.ShapeDtypeStruct((t, c), BF16), jax.ShapeDtypeStruct((t, POOL_WIDTH), BF16)],
        grid_spec=pltpu.PrefetchScalarGridSpec(
            num_scalar_prefetch=1,
            grid=(t // tl,),
            in_specs=[cur(c), prev(c), nxt(c), cur(POOL_WIDTH), prev(POOL_WIDTH), nxt(POOL_WIDTH)]
                     + [cst(a) for a in consts],
            out_specs=[cur(c), cur(POOL_WIDTH)],
            scratch_shapes=[pltpu.VMEM((tl + 2 * HALO, c), F32)]),
        compiler_params=_cparams(("arbitrary",)),
        name="local_mixers",
    )(table, xbc, xbc, xbc, u, u, u, *consts)


def _ssd_direction(xbc_ref, dt_ref, alog_ref, expand_ref, state_ref, y_out, reverse):
    n_sub = xbc_ref.shape[0] // SSD_CHUNK
    for sub in (reversed(range(n_sub)) if reverse else range(n_sub)):
        rows = slice(sub * SSD_CHUNK, (sub + 1) * SSD_CHUNK)
        _ssd_chunk(xbc_ref, dt_ref, alog_ref, expand_ref, state_ref, y_out, rows, reverse)


def _ssd_chunk(xbc_ref, dt_ref, alog_ref, expand_ref, state_ref, y_out, rows, reverse):
    cl = SSD_CHUNK
    gw = SSM_INNER // SSM_GROUPS
    lane0 = SSM_HEADS if reverse else 0
    xs = xbc_ref[rows, :SSM_INNER].astype(F32)
    bm = xbc_ref[rows, SSM_INNER:SSM_INNER + SSM_GROUPS * SSM_STATE]
    cm = xbc_ref[rows, SSM_INNER + SSM_GROUPS * SSM_STATE:]
    dt = dt_ref[rows, :]
    a = dt * (-jnp.exp(alog_ref[...]))
    ri = lax.broadcasted_iota(jnp.int32, (cl, cl), 0)
    ci = lax.broadcasted_iota(jnp.int32, (cl, cl), 1)
    tri = (ci >= ri) if reverse else (ci <= ri)
    trib = tri.astype(BF16)
    acum = sum(jnp.dot(trib, part, preferred_element_type=F32) for part in _split3(a))
    acum_t = acum.T
    last = 0 if reverse else cl - 1
    atot = acum[last:last + 1, :]
    stacked = jnp.concatenate([dt, jnp.exp(acum), jnp.exp(atot - acum)], axis=0)
    ex = sum(jnp.dot(part, expand_ref[...], preferred_element_type=F32) for part in _split2(stacked))
    dt_e, eacum_e, wend_e = ex[:cl], ex[cl:2 * cl], ex[2 * cl:]
    xdt = xs * dt_e
    xdt_b = xdt.astype(BF16)
    lane = lax.broadcasted_iota(jnp.int32, (1, LANES), 1)
    left = lane < SSM_HEAD_DIM
    state = state_ref[...]
    state_b = state.astype(BF16)
    y_parts = []
    for g in range(SSM_GROUPS):
        bg = bm[:, g * SSM_STATE:(g + 1) * SSM_STATE]
        cg = cm[:, g * SSM_STATE:(g + 1) * SSM_STATE]
        cb = _nt_dot(cg, bg)
        y_off = jnp.dot(cg, state_b[:, g * gw:(g + 1) * gw], preferred_element_type=F32)
        heads_g = SSM_HEADS // SSM_GROUPS
        diag = []
        for pair in range(heads_g // 2):
            ms = []
            for hh in range(2):
                hl = lane0 + g * heads_g + pair * 2 + hh
                seg = acum[:, hl:hl + 1] - acum_t[hl:hl + 1, :]
                ms.append((cb * jnp.where(tri, jnp.exp(jnp.minimum(seg, 0.0)), 0.0)).astype(BF16))
            c0 = g * gw + pair * LANES
            xp = xdt_b[:, c0:c0 + LANES]
            rhs = jnp.concatenate([jnp.where(left, xp, jnp.zeros_like(xp)),
                                   jnp.where(left, jnp.zeros_like(xp), xp)], axis=0)
            diag.append(jnp.dot(jnp.concatenate(ms, axis=1), rhs, preferred_element_type=F32))
        y_parts.append(jnp.concatenate(diag, axis=1) + y_off * eacum_e[:, g * gw:(g + 1) * gw])
        xw = (xdt[:, g * gw:(g + 1) * gw] * wend_e[:, g * gw:(g + 1) * gw]).astype(BF16)
        upd = jnp.dot(bg.astype(F32).T.astype(BF16), xw, preferred_element_type=F32)
        state_ref[:, g * gw:(g + 1) * gw] = (state[:, g * gw:(g + 1) * gw]
                                             * eacum_e[last:last + 1, g * gw:(g + 1) * gw] + upd)
    y_out[rows, :] = jnp.concatenate(y_parts, axis=1).astype(y_out.dtype)


def _ssd_kernel(tab_ref, xf_ref, xb_ref, dtf_ref, dtb_ref, alog_ref, ef_ref, eb_ref, h0f_ref, h0b_ref,
                yf_out, yb_out, sf_out, sb_out, state_f, state_b):
    g = pl.program_id(0)

    @pl.when(tab_ref[1, g] == 1)
    def _():
        state_f[...] = h0f_ref[...].T
        state_b[...] = h0b_ref[...].T

    _ssd_direction(xf_ref, dtf_ref, alog_ref, ef_ref, state_f, yf_out, reverse=False)
    _ssd_direction(xb_ref, dtb_ref, alog_ref, eb_ref, state_b, yb_out, reverse=True)

    @pl.when(tab_ref[2, g] == 1)
    def _():
        sf_out[...] = state_f[...].T
        sb_out[...] = state_b[...].T


def _ssd_tables(seq_lens, cl):
    bwd, first, last, seq = [], [], [], []
    start = 0
    for s, sl in enumerate(seq_lens):
        nc = sl // cl
        for c in range(nc):
            bwd.append(start + nc - 1 - c)
            first.append(int(c == 0))
            last.append(int(c == nc - 1))
            seq.append(s)
        start += nc
    return np.asarray([bwd, first, last, seq], np.int32)


def ssd_scan(xbc, dt, a_log, h0_f, h0_b, seq_lens):
    t = xbc.shape[0]
    cl = SSD_BLOCK
    n_seq = len(seq_lens)
    tab = jnp.asarray(_ssd_tables(seq_lens, cl))
    alog = jnp.pad(a_log.reshape(1, 2 * SSM_HEADS), ((0, 0), (0, LANES - 2 * SSM_HEADS)))
    ef = np.zeros((LANES, SSM_INNER), np.float32)
    eb = np.zeros((LANES, SSM_INNER), np.float32)
    for h in range(SSM_HEADS):
        ef[h, h * SSM_HEAD_DIM:(h + 1) * SSM_HEAD_DIM] = 1.0
        eb[SSM_HEADS + h, h * SSM_HEAD_DIM:(h + 1) * SSM_HEAD_DIM] = 1.0
    fwd = lambda w: pl.BlockSpec((cl, w), lambda g, tb: (g, 0))
    bwd = lambda w: pl.BlockSpec((cl, w), lambda g, tb: (tb[0, g], 0))
    cst = lambda shape: pl.BlockSpec(shape, lambda g, tb: (0,) * len(shape), pipeline_mode=pl.Buffered(1))
    st = lambda: pl.BlockSpec((None, SSM_INNER, SSM_STATE), lambda g, tb: (tb[3, g], 0, 0))
    return pl.pallas_call(
        _ssd_kernel,
        out_shape=[jax.ShapeDtypeStruct((t, SSM_INNER), BF16), jax.ShapeDtypeStruct((t, SSM_INNER), BF16),
                   jax.ShapeDtypeStruct((n_seq, SSM_INNER, SSM_STATE), F32),
                   jax.ShapeDtypeStruct((n_seq, SSM_INNER, SSM_STATE), F32)],
        grid_spec=pltpu.PrefetchScalarGridSpec(
            num_scalar_prefetch=1,
            grid=(t // cl,),
            in_specs=[fwd(SSM_CONV_CH), bwd(SSM_CONV_CH), fwd(LANES), bwd(LANES),
                      cst((1, LANES)), cst((LANES, SSM_INNER)), cst((LANES, SSM_INNER)), st(), st()],
            out_specs=[fwd(SSM_INNER), bwd(SSM_INNER), st(), st()],
            scratch_shapes=[pltpu.VMEM((SSM_STATE, SSM_INNER), F32), pltpu.VMEM((SSM_STATE, SSM_INNER), F32)]),
        compiler_params=_cparams(("arbitrary",)),
        name="ssd_scan",
    )(tab, xbc, xbc, dt, dt, alog, jnp.asarray(ef, BF16), jnp.asarray(eb, BF16), h0_f, h0_b)


def _merge_kernel(x_ref, mod_ref, a_ref, yf_ref, yb_ref, xs_ref, z_ref, p_ref, g_ref,
                  dsum_ref, snw_ref, wna_ref, wssm_ref, wpool_ref, wout_ref, n2_ref, *rest,
                  with_router):
    if with_router:
        router_ref, x1_out, h2_out, logit_out = rest
    else:
        x1_out, h2_out = rest
    d = x_ref.shape[1]
    y = yf_ref[...].astype(F32) + yb_ref[...].astype(F32) + dsum_ref[...] * xs_ref[...].astype(F32)
    y = y * _silu(z_ref[...].astype(F32))
    gw = SSM_INNER // SSM_GROUPS
    parts = []
    for g in range(SSM_GROUPS):
        yg = y[:, g * gw:(g + 1) * gw]
        yg = yg * lax.rsqrt(jnp.mean(yg * yg, axis=-1, keepdims=True) + NORM_EPS)
        parts.append(yg * snw_ref[:, g * gw:(g + 1) * gw])
    s = jnp.concatenate(parts, axis=1).astype(BF16)
    gate = lambda j: g_ref[:, j * d:(j + 1) * d].astype(F32)
    merged = (gate(0) * jnp.dot(a_ref[...], wna_ref[...], preferred_element_type=F32)
              + gate(1) * jnp.dot(s, wssm_ref[...], preferred_element_type=F32)
              + gate(2) * jnp.dot(p_ref[...], wpool_ref[...], preferred_element_type=F32))
    x1 = x_ref[...] + mod_ref[2:3, :] * jnp.dot(merged.astype(BF16), wout_ref[...],
                                                preferred_element_type=F32)
    x1_out[...] = x1
    h2 = _modulated_norm(x1, n2_ref[...], mod_ref[3:4, :], mod_ref[4:5, :])
    if with_router:
        _store_token_major(h2_out, h2)
        hh, hl = _split2(h2)
        rh, rl = router_ref[0], router_ref[1]
        logit_out[...] = (jnp.dot(hh, rh, preferred_element_type=F32)
                          + jnp.dot(hl, rh, preferred_element_type=F32)
                          + jnp.dot(hh, rl, preferred_element_type=F32))
    else:
        h2_out[...] = h2.astype(BF16)


def merge_branches(x, mod, a, yf, yb, xbc, z, p, gates, d_sum, ssm_norm_w, wb_na, wb_ssm, wb_pool, w_out,
                   norm2_w, router, n_ctx_tok, lat_len, tm=512):
    t, d = x.shape
    row = lambda i: (i, 0)
    consts = [jnp.repeat(d_sum, SSM_HEAD_DIM).reshape(1, SSM_INNER), ssm_norm_w.reshape(1, SSM_INNER),
              wb_na.astype(BF16), wb_ssm.astype(BF16), wb_pool.astype(BF16), w_out.astype(BF16),
              norm2_w.reshape(1, d)]
    out_shape = [jax.ShapeDtypeStruct((t, d), F32)]
    out_specs = [pl.BlockSpec((tm, d), row)]
    with_router = router is not None
    if with_router:
        rpad = jnp.pad(router, ((0, 0), (0, LANES - router.shape[1])))
        rh = rpad.astype(BF16)
        rl = (rpad - rh.astype(F32)).astype(BF16)
        consts.append(jnp.stack([rh, rl]))
        out_shape += [jax.ShapeDtypeStruct((t * ROW_TILES, LANES), F32), jax.ShapeDtypeStruct((t, LANES), F32)]
        out_specs += [pl.BlockSpec((tm * ROW_TILES, LANES), row), pl.BlockSpec((tm, LANES), row)]
    else:
        out_shape += [jax.ShapeDtypeStruct((t, d), BF16)]
        out_specs += [pl.BlockSpec((tm, d), row)]
    return pl.pallas_call(
        functools.partial(_merge_kernel, with_router=with_router),
        out_shape=out_shape,
        grid=(t // tm,),
        in_specs=[pl.BlockSpec((tm, d), row),
                  pl.BlockSpec((None, 6, d), lambda i: (_mod_row(i, tm, n_ctx_tok, lat_len), 0, 0)),
                  pl.BlockSpec((tm, NA_WIDTH), row), pl.BlockSpec((tm, SSM_INNER), row),
                  pl.BlockSpec((tm, SSM_INNER), row), pl.BlockSpec((tm, SSM_INNER), row),
                  pl.BlockSpec((tm, SSM_INNER), row), pl.BlockSpec((tm, POOL_WIDTH), row),
                  pl.BlockSpec((tm, N_BRANCH * d), row)]
                 + [_const_spec(c.shape) for c in consts],
        out_specs=out_specs,
        compiler_params=_cparams(("parallel",)),
        name="merge_branches",
    )(x, mod, a, yf, yb, xbc, z, p, gates, *consts)


def _ffn_kernel(h_ref, x_ref, mod_ref, w1_ref, w3_ref, w2_ref, o_ref, acc_ref, *, tf):
    f = w1_ref.shape[1]
    hb = h_ref[...]
    for c in range(f // tf):
        sl = slice(c * tf, (c + 1) * tf)
        u = (_silu(jnp.dot(hb, w1_ref[:, sl], preferred_element_type=F32))
             * jnp.dot(hb, w3_ref[:, sl], preferred_element_type=F32))
        part = jnp.dot(u.astype(BF16), w2_ref[sl, :], preferred_element_type=F32)
        if c == 0:
            acc_ref[...] = part
        else:
            acc_ref[...] += part
    o_ref[...] = x_ref[...] + mod_ref[5:6, :] * acc_ref[...]


def dense_ffn(h2, x1, mod, w1, w3, w2, n_ctx_tok, lat_len, tm=512, tf=256):
    t, d = x1.shape
    consts = [w1.astype(BF16), w3.astype(BF16), w2.astype(BF16)]
    return pl.pallas_call(
        functools.partial(_ffn_kernel, tf=tf),
        out_shape=jax.ShapeDtypeStruct((t, d), F32),
        grid=(t // tm,),
        in_specs=[pl.BlockSpec((tm, d), lambda i: (i, 0)), pl.BlockSpec((tm, d), lambda i: (i, 0)),
                  pl.BlockSpec((None, 6, d), lambda i: (_mod_row(i, tm, n_ctx_tok, lat_len), 0, 0))]
                 + [_const_spec(c.shape) for c in consts],
        out_specs=pl.BlockSpec((tm, d), lambda i: (i, 0)),
        scratch_shapes=[pltpu.VMEM((tm, d), F32)],
        compiler_params=_cparams(("parallel",)),
        name="dense_ffn",
    )(h2, x1, mod, *consts)


def _route_kernel(logit_ref, info_out, gate_out, count_out, base_ref):
    i = pl.program_id(0)

    @pl.when(i == 0)
    def _():
        base_ref[...] = jnp.zeros_like(base_ref)

    lg = logit_ref[...]
    tm = lg.shape[0]
    lane = lax.broadcasted_iota(jnp.int32, lg.shape, 1)
    valid = lane < N_EXPERTS
    lg = jnp.where(valid, lg, -jnp.inf)
    m1 = jnp.max(lg, axis=-1, keepdims=True)
    e1 = jnp.min(jnp.where(lg == m1, lane, LANES), axis=-1, keepdims=True)
    lg2 = jnp.where(lane == e1, -jnp.inf, lg)
    m2 = jnp.max(lg2, axis=-1, keepdims=True)
    e2 = jnp.min(jnp.where(lg2 == m2, lane, LANES), axis=-1, keepdims=True)
    g1 = 1.0 / (1.0 + jnp.exp(m2 - m1))
    oh1 = (lane == e1).astype(F32)
    oh2 = (lane == e2).astype(F32)
    oh = oh1 + oh2
    ri = lax.broadcasted_iota(jnp.int32, (tm, tm), 0)
    ci = lax.broadcasted_iota(jnp.int32, (tm, tm), 1)
    before = jnp.dot((ci < ri).astype(BF16), oh.astype(BF16), preferred_element_type=F32) + base_ref[...]
    r1 = jnp.sum(oh1 * before, axis=-1, keepdims=True)
    r2 = jnp.sum(oh2 * before, axis=-1, keepdims=True)
    lane8 = lax.broadcasted_iota(jnp.int32, (tm, 8), 1)
    info = jnp.where(lane8 == 0, e1, jnp.where(lane8 == 1, e2,
                     jnp.where(lane8 == 2, r1.astype(jnp.int32), r2.astype(jnp.int32))))
    info_out[...] = info
    gate_out[...] = jnp.where(lane8 == 0, g1, 1.0 - g1)
    base_ref[...] += jnp.sum(oh, axis=0, keepdims=True)
    count_out[...] = base_ref[...].astype(jnp.int32)


def route_tokens(logits, tm=512):
    t = logits.shape[0]
    return pl.pallas_call(
        _route_kernel,
        out_shape=[jax.ShapeDtypeStruct((t, 8), jnp.int32), jax.ShapeDtypeStruct((t, 8), F32),
                   jax.ShapeDtypeStruct((1, LANES), jnp.int32)],
        grid=(t // tm,),
        in_specs=[pl.BlockSpec((tm, LANES), lambda i: (i, 0))],
        out_specs=[pl.BlockSpec((tm, 8), lambda i: (i, 0)), pl.BlockSpec((tm, 8), lambda i: (i, 0)),
                   pl.BlockSpec((1, LANES), lambda i: (0, 0))],
        scratch_shapes=[pltpu.VMEM((1, LANES), F32)],
        compiler_params=_cparams(("arbitrary",)),
        name="route_tokens",
    )(logits)


def _token_rows(idx):
    return pl.ds(pl.multiple_of(idx * ROW_TILES, ROW_TILES), ROW_TILES)


def _dma_loop(copy, n, wait):
    def body(i, c):
        if wait:
            copy(i).wait()
        else:
            copy(i).start()
        return c

    lax.fori_loop(0, n, body, 0, unroll=8)


def _expert_kernel(ge_ref, tcur_ref, tnext_ref, dprev_ref, h_ref, w1_ref, w3_ref, w2_ref, y_ref,
                   ibuf, obuf, xs_ref, acc_ref, gsem, ssem):
    g = pl.program_id(0)
    f = pl.program_id(1)
    n_groups = pl.num_programs(0) - 1
    tg = xs_ref.shape[0]
    live = g < n_groups

    def gather(tok_ref, slot, n):
        return pltpu.make_async_copy(h_ref.at[_token_rows(tok_ref[0, n])], ibuf.at[slot, _token_rows(n)],
                                     gsem.at[slot])

    def scatter(n):
        return pltpu.make_async_copy(obuf.at[_token_rows(n)], y_ref.at[_token_rows(dprev_ref[0, n])], ssem)

    def partial_out():
        xb = xs_ref[...]
        u = (_silu(jnp.dot(xb, w1_ref[...], preferred_element_type=F32))
             * jnp.dot(xb, w3_ref[...], preferred_element_type=F32))
        return jnp.dot(u.astype(BF16), w2_ref[...], preferred_element_type=F32)

    @pl.when((g == 0) & (f == 0))
    def _():
        obuf[...] = jnp.zeros_like(obuf)
        _dma_loop(lambda n: gather(tcur_ref, 0, n), tg, wait=False)

    @pl.when(live & (f == 0))
    def _():
        for n in range(tg):
            gather(tcur_ref, g % 2, n).wait()
        for k in range(ROW_TILES):
            xs_ref[:, k * LANES:(k + 1) * LANES] = _load_token_major(ibuf.at[g % 2], tg, k).astype(BF16)
        for n in range(tg):
            scatter(n).start()
        acc_ref[...] = partial_out()

    @pl.when(jnp.logical_not(live) & (f == 0))
    def _():
        _dma_loop(lambda n: gather(tcur_ref, g % 2, n), tg, wait=True)
        _dma_loop(scatter, tg, wait=False)

    @pl.when(live & (f == 1))
    def _():
        for n in range(tg):
            scatter(n).wait()
        for n in range(tg):
            gather(tnext_ref, (g + 1) % 2, n).start()
        _store_token_major(obuf, acc_ref[...] + partial_out())

    @pl.when(jnp.logical_not(live) & (f == 1))
    def _():
        _dma_loop(scatter, tg, wait=True)


def expert_ffn(h2, slot_tok, slot_dst, group_expert, n_rows_out, w1, w3, w2):
    n_slots = slot_tok.shape[0]
    d, ff = w1.shape[1], w1.shape[2]
    tf = ff // 2
    tg = MOE_GROUP
    ng = n_slots // tg
    tok_tiles = slot_tok.reshape(ng, 1, tg)
    spare = n_rows_out - tg + jnp.arange(tg, dtype=jnp.int32)
    dst_tiles = jnp.concatenate([spare, slot_dst]).reshape(ng + 1, 1, tg)
    smem = lambda imap: pl.BlockSpec((None, 1, tg), imap, memory_space=pltpu.SMEM)
    gi = lambda g: jnp.minimum(g, ng - 1)
    return pl.pallas_call(
        _expert_kernel,
        out_shape=jax.ShapeDtypeStruct((n_rows_out * ROW_TILES, LANES), F32),
        grid_spec=pltpu.PrefetchScalarGridSpec(
            num_scalar_prefetch=1,
            grid=(ng + 1, 2),
            in_specs=[smem(lambda g, f, ge: (gi(g), 0, 0)),
                      smem(lambda g, f, ge: (gi(g + 1), 0, 0)),
                      smem(lambda g, f, ge: (g, 0, 0)),
                      pl.BlockSpec(memory_space=pl.ANY),
                      pl.BlockSpec((None, d, tf), lambda g, f, ge: (ge[gi(g)], 0, f)),
                      pl.BlockSpec((None, d, tf), lambda g, f, ge: (ge[gi(g)], 0, f)),
                      pl.BlockSpec((None, tf, d), lambda g, f, ge: (ge[gi(g)], f, 0))],
            out_specs=pl.BlockSpec(memory_space=pl.ANY),
            scratch_shapes=[pltpu.VMEM((2, tg * ROW_TILES, LANES), F32), pltpu.VMEM((tg * ROW_TILES, LANES), F32),
                            pltpu.VMEM((tg, d), BF16), pltpu.VMEM((tg, d), F32),
                            pltpu.SemaphoreType.DMA((2,)), pltpu.SemaphoreType.DMA(())]),
        compiler_params=_cparams(("arbitrary", "arbitrary")),
        name="expert_ffn",
    )(group_expert, tok_tiles, tok_tiles, dst_tiles, h2, w1.astype(BF16), w3.astype(BF16), w2.astype(BF16))


def _combine_kernel(y_ref, x_ref, mod_ref, gate_ref, oc_ref, ol_ref, *, n_ctx_tiles):
    tm = x_ref.shape[0]
    g0, g1 = gate_ref[:, 0:1], gate_ref[:, 1:2]

    def emit(o_ref):
        for k in range(ROW_TILES):
            sl = slice(k * LANES, (k + 1) * LANES)
            y = (y_ref[pl.ds(k, tm, stride=2 * ROW_TILES), :] * g0
                 + y_ref[pl.ds(ROW_TILES + k, tm, stride=2 * ROW_TILES), :] * g1)
            o_ref[:, sl] = x_ref[:, sl] + mod_ref[5:6, sl] * y

    @pl.when(pl.program_id(0) < n_ctx_tiles)
    def _():
        emit(oc_ref)

    @pl.when(pl.program_id(0) >= n_ctx_tiles)
    def _():
        emit(ol_ref)


def combine_rows(y2, gates, x1, mod, n_ctx_tok, lat_len, tm=256):
    t, d = x1.shape
    nct = n_ctx_tok // tm
    return pl.pallas_call(
        functools.partial(_combine_kernel, n_ctx_tiles=nct),
        out_shape=[jax.ShapeDtypeStruct((n_ctx_tok, d), F32), jax.ShapeDtypeStruct((t - n_ctx_tok, d), F32)],
        grid=(t // tm,),
        in_specs=[pl.BlockSpec((2 * tm * ROW_TILES, LANES), lambda i: (i, 0)),
                  pl.BlockSpec((tm, d), lambda i: (i, 0)),
                  pl.BlockSpec((None, 6, d), lambda i: (_mod_row(i, tm, n_ctx_tok, lat_len), 0, 0)),
                  pl.BlockSpec((tm, 8), lambda i: (i, 0))],
        out_specs=[pl.BlockSpec((tm, d), lambda i: (jnp.minimum(i, nct - 1), 0)),
                   pl.BlockSpec((tm, d), lambda i: (jnp.maximum(i - nct, 0), 0))],
        compiler_params=_cparams(("arbitrary",)),
        name="moe_combine",
    )(y2, x1, mod, gates)


def moe_ffn(h2, logits, x1, mod, w1, w3, w2, n_ctx_tok, lat_len):
    t = x1.shape[0]
    info, gates, counts = route_tokens(logits)
    counts = counts[0, :N_EXPERTS]
    padded = (counts + MOE_GROUP - 1) // MOE_GROUP * MOE_GROUP
    pad_end = jnp.cumsum(padded)
    pad_start = pad_end - padded
    experts = info[:, 0:2]
    dest = jnp.sum(jnp.where(experts[..., None] == jnp.arange(N_EXPERTS), pad_start, 0), axis=-1) + info[:, 2:4]
    n_groups = 2 * t // MOE_GROUP + N_EXPERTS
    n_slots = n_groups * MOE_GROUP
    group_start = jnp.arange(n_groups, dtype=jnp.int32) * MOE_GROUP
    group_expert = jnp.minimum(jnp.sum(group_start[:, None] >= pad_end[None, :], axis=-1), N_EXPERTS - 1)
    slot_asg = jnp.full((n_slots,), -1, jnp.int32).at[dest.reshape(-1)].set(
        jnp.arange(2 * t, dtype=jnp.int32), unique_indices=True, mode="promise_in_bounds")
    is_pad = (slot_asg < 0).astype(jnp.int32)
    slot_dst = jnp.where(slot_asg >= 0, slot_asg, 2 * t - 1 + jnp.cumsum(is_pad))
    slot_tok = jnp.maximum(slot_asg, 0) // 2
    y2 = expert_ffn(h2, slot_tok, slot_dst, group_expert.astype(jnp.int32), n_slots + MOE_GROUP, w1, w3, w2)
    return combine_rows(y2, gates, x1, mod, n_ctx_tok, lat_len)


def kernel(x_prompt, x_sample, c, cache_na_k, cache_na_v, state_ssd_fwd, state_ssd_bwd, c_ctx, norm1_w, norm2_w, ada_w, ada_b, w_in, na_q_norm, na_k_norm, na_rpb, ssm_conv_w, ssm_conv_b, ssm_dt_bias, ssm_a_log, ssm_d, ssm_norm_w, pool_w, pool_scale, w_branch_na, w_branch_ssm, w_branch_pool, w_gate, b_gate, w_out, ffn_w1, ffn_w3, ffn_w2, moe_router, moe_w1, moe_w3, moe_w2):
    n_ctx, ctx_len, d = x_prompt.shape
    n_lat, lat_len, _ = x_sample.shape
    depth = ada_w.shape[0]
    n_ctx_tok = n_ctx * ctx_len
    seq_lens = (ctx_len,) * n_ctx + (lat_len,) * n_lat
    lc = cache_na_k.shape[2]

    x = jnp.concatenate([x_prompt.reshape(n_ctx_tok, d), x_sample.reshape(n_lat * lat_len, d)], axis=0)
    cond = jnp.concatenate([c_ctx[None, :], c], axis=0)
    cond = jnp.pad(cond, ((0, -cond.shape[0] % 8), (0, 0)))
    mods = ada_modulation(cond, ada_w, ada_b).reshape(depth, cond.shape[0], 6, d)
    zero_state = jnp.zeros((n_ctx, SSM_INNER, SSM_STATE), F32)

    ks, vs, sfs, sbs = [], [], [], []
    for i in range(depth):
        mod = mods[i]
        q, k, v, kb, vb, z, xbc, dt, u = input_projection(
            x, mod, norm1_w[i], w_in[i], na_q_norm[i], na_k_norm[i], ssm_dt_bias[i], n_ctx_tok, lat_len)
        gates = branch_gates(x, mod, norm1_w[i], w_gate[i], b_gate[i], n_ctx_tok, lat_len)
        a_ctx = context_attention(q, kb, vb, n_ctx, ctx_len)
        a_lat = latent_attention(q, kb, vb, cache_na_k[:, i].reshape(n_lat, lc, NA_WIDTH),
                                 cache_na_v[:, i].reshape(n_lat, lc, NA_WIDTH), na_rpb[i],
                                 n_ctx_tok, n_lat, lat_len)
        a = jnp.concatenate([a_ctx, a_lat], axis=0)
        xbc_c, p = local_mixers(xbc, u, ssm_conv_w[i], ssm_conv_b[i], pool_w[i], pool_scale[i], seq_lens)
        h0_f = jnp.concatenate([zero_state, state_ssd_fwd[:, i].reshape(n_lat, SSM_INNER, SSM_STATE)], axis=0)
        h0_b = jnp.concatenate([zero_state, state_ssd_bwd[:, i].reshape(n_lat, SSM_INNER, SSM_STATE)], axis=0)
        yf, yb, s_f, s_b = ssd_scan(xbc_c, dt, ssm_a_log[i], h0_f, h0_b, seq_lens)
        j = i // 2
        router = moe_router[j] if i % 2 == 1 else None
        res = merge_branches(x, mod, a, yf, yb, xbc_c, z, p, gates, ssm_d[i, 0] + ssm_d[i, 1],
                             ssm_norm_w[i], w_branch_na[i], w_branch_ssm[i], w_branch_pool[i], w_out[i],
                             norm2_w[i], router, n_ctx_tok, lat_len)
        if i % 2 == 0:
            x1, h2 = res
            x = dense_ffn(h2, x1, mod, ffn_w1[j], ffn_w3[j], ffn_w2[j], n_ctx_tok, lat_len)
        else:
            x1, h2, logits = res
            x_parts = moe_ffn(h2, logits, x1, mod, moe_w1[j], moe_w3[j], moe_w2[j], n_ctx_tok, lat_len)
            x = jnp.concatenate(x_parts, axis=0) if i + 1 < depth else None
        ks.append(k[:n_ctx_tok].reshape(n_ctx, ctx_len, NA_HEADS, NA_HEAD_DIM))
        vs.append(v[:n_ctx_tok].reshape(n_ctx, ctx_len, NA_HEADS, NA_HEAD_DIM))
        sfs.append(s_f[:n_ctx].reshape(n_ctx, SSM_HEADS, SSM_HEAD_DIM, SSM_STATE))
        sbs.append(s_b[:n_ctx].reshape(n_ctx, SSM_HEADS, SSM_HEAD_DIM, SSM_STATE))

    if x is None:
        x_ctx, x_lat = x_parts
    else:
        x_ctx, x_lat = x[:n_ctx_tok], x[n_ctx_tok:]
    y_prompt = x_ctx.reshape(n_ctx, ctx_len, d)
    y_sample = x_lat.reshape(n_lat, lat_len, d)
    return (y_prompt, y_sample, jnp.stack(ks, axis=1), jnp.stack(vs, axis=1),
            jnp.stack(sfs, axis=1), jnp.stack(sbs, axis=1))
```
